```python
import jax
import jax.numpy as jnp
from jax import lax
import numpy as np

D_MODEL = 2048
BATCH = 2
SEQ = 4096
DEPTH = 2
DEC_BATCH = 4
DEC_SEQ = 2048
PAST_LEN = 128

GRID_W = 64
PLE_DIM = 256
D_FF = 4 * D_MODEL
ROPE_THETA = 10000.0
EPS = 1e-6

MLA_HEADS = 8
MLA_NOPE = 128
MLA_ROPE = 64
MLA_QK = MLA_NOPE + MLA_ROPE
MLA_V = 128
Q_LORA = 512
KV_LORA = 256
MLA_QBLOCK = 128

NA_HEADS = 8
NA_HEAD_DIM = 128
NA_KH_MAX = 8
NA_KW = 16

SWA_HEADS = 16
SWA_KV_HEADS = 4
SWA_HEAD_DIM = 128
SWA_WINDOW = 128
SWA_BLOCK = 128

EVEN_IN = Q_LORA + KV_LORA + MLA_ROPE + 3 * NA_HEADS * NA_HEAD_DIM
EVEN_MIX = MLA_HEADS * MLA_V + NA_HEADS * NA_HEAD_DIM
ODD_IN = (SWA_HEADS + 2 * SWA_KV_HEADS) * SWA_HEAD_DIM
ODD_MIX = SWA_HEADS * SWA_HEAD_DIM
N_EVEN = (DEPTH + 1) // 2
N_ODD = DEPTH // 2

kernel_name = 'hybrid_mla_natten_swa_encoder'


def rms_norm(x, g):
    xf = x.astype(jnp.float32)
    y = xf * lax.rsqrt(jnp.mean(xf * xf, axis=-1, keepdims=True) + EPS)
    return (y * g.astype(jnp.float32)).astype(x.dtype)


def rope_tables(T, dim):
    inv_freq = 1.0 / (ROPE_THETA ** (jnp.arange(0, dim, 2, dtype=jnp.float32) / dim))
    ang = jnp.arange(T, dtype=jnp.float32)[:, None] * inv_freq[None, :]
    return jnp.cos(ang), jnp.sin(ang)


def apply_rope(x, cos, sin):
    half = x.shape[-1] // 2
    x1, x2 = x[..., :half], x[..., half:]
    c = cos[:, None, :].astype(x.dtype)
    s = sin[:, None, :].astype(x.dtype)
    return jnp.concatenate([x1 * c - x2 * s, x2 * c + x1 * s], axis=-1)


def mla_attention(q_lat, kv_lat, k_rope, q_a_norm, w_q_b, kv_a_norm, w_kv_b,
                  q_nope_norm, q_rope_norm, k_nope_norm, k_rope_norm):
    B, T, _ = q_lat.shape
    q = (rms_norm(q_lat, q_a_norm) @ w_q_b).reshape(B, T, MLA_HEADS, MLA_QK)
    kv = (rms_norm(kv_lat, kv_a_norm) @ w_kv_b).reshape(B, T, MLA_HEADS, MLA_NOPE + MLA_V)
    cos, sin = rope_tables(T, MLA_ROPE)
    q_nope = rms_norm(q[..., :MLA_NOPE], q_nope_norm)
    q_rope = apply_rope(rms_norm(q[..., MLA_NOPE:], q_rope_norm), cos, sin)
    k_nope = rms_norm(kv[..., :MLA_NOPE], k_nope_norm)
    v = kv[..., MLA_NOPE:]
    k_r = apply_rope(rms_norm(k_rope, k_rope_norm)[:, :, None, :], cos, sin)[:, :, 0, :]
    scale = MLA_QK ** -0.5
    nb = T // MLA_QBLOCK
    qn_blocks = q_nope.reshape(B, nb, MLA_QBLOCK, MLA_HEADS, MLA_NOPE).transpose(1, 0, 2, 3, 4)
    qr_blocks = q_rope.reshape(B, nb, MLA_QBLOCK, MLA_HEADS, MLA_ROPE).transpose(1, 0, 2, 3, 4)

    def attend_block(blk):
        qn, qr = blk
        s = jnp.einsum('bqhd,bkhd->bhqk', qn, k_nope) + jnp.einsum('bqhd,bkd->bhqk', qr, k_r)
        p = jax.nn.softmax(s.astype(jnp.float32) * scale, axis=-1).astype(v.dtype)
        return jnp.einsum('bhqk,bkhd->bqhd', p, v)

    o = lax.map(attend_block, (qn_blocks, qr_blocks))
    return o.transpose(1, 0, 2, 3, 4).reshape(B, T, MLA_HEADS * MLA_V)


def neighbourhood_attention(q, k, v, rpb):
    B, T, H, dh = q.shape
    rows = T // GRID_W
    kh = min(NA_KH_MAX, rows)
    col = jnp.arange(GRID_W)
    c_start = jnp.clip(col - NA_KW // 2, 0, GRID_W - NA_KW)
    kcol = c_start[:, None] + jnp.arange(NA_KW)[None, :]
    dc_idx = kcol - col[:, None] + (NA_KW - 1)
    scale = dh ** -0.5
    q_rows = q.reshape(B, rows, GRID_W, H, dh).transpose(1, 0, 2, 3, 4)

    def row_block(args):
        r, q_row = args
        r_start = jnp.clip(r - kh // 2, 0, rows - kh)
        krow = r_start + jnp.arange(kh)
        idx = krow[None, :, None] * GRID_W + kcol[:, None, :]
        kg = k[:, idx]
        vg = v[:, idx]
        dr_idx = krow - r + (NA_KH_MAX - 1)
        bias = rpb[:, dr_idx[None, :, None], dc_idx[:, None, :]]
        s = jnp.einsum('bwhd,bwijhd->bhwij', q_row, kg).astype(jnp.float32) * scale
        s = s + bias[None].astype(jnp.float32)
        p = jax.nn.softmax(s.reshape(B, H, GRID_W, kh * NA_KW), axis=-1)
        p = p.reshape(B, H, GRID_W, kh, NA_KW).astype(v.dtype)
        return jnp.einsum('bhwij,bwijhd->bwhd', p, vg)

    o = lax.map(row_block, (jnp.arange(rows), q_rows))
    return o.transpose(1, 0, 2, 3, 4).reshape(B, T, H * dh)


def sliding_window_attention(q, k, v, sinks):
    B, T, H, dh = q.shape
    hkv = k.shape[2]
    G = H // hkv
    nb = T // SWA_BLOCK
    qb = q.reshape(B, nb, SWA_BLOCK, hkv, G, dh)
    pad = ((0, 0), (SWA_BLOCK, SWA_BLOCK), (0, 0), (0, 0))
    kp = jnp.pad(k, pad).reshape(B, nb + 2, SWA_BLOCK, hkv, dh)
    vp = jnp.pad(v, pad).reshape(B, nb + 2, SWA_BLOCK, hkv, dh)
    kw = jnp.concatenate([kp[:, :-2], kp[:, 1:-1], kp[:, 2:]], axis=2)
    vw = jnp.concatenate([vp[:, :-2], vp[:, 1:-1], vp[:, 2:]], axis=2)
    qpos = jnp.arange(T).reshape(nb, SWA_BLOCK)
    kpos = (jnp.arange(nb)[:, None] - 1) * SWA_BLOCK + jnp.arange(3 * SWA_BLOCK)[None, :]
    mask = ((jnp.abs(qpos[:, :, None] - kpos[:, None, :]) <= SWA_WINDOW)
            & (kpos[:, None, :] >= 0) & (kpos[:, None, :] < T))
    scale = dh ** -0.5
    s = jnp.einsum('bnqkgd,bnskd->bnkgqs', qb, kw).astype(jnp.float32) * scale
    s = jnp.where(mask[None, :, None, None], s, -jnp.inf)
    sink = jnp.broadcast_to(sinks.astype(jnp.float32).reshape(hkv, G)[None, None, :, :, None, None],
                            s.shape[:-1] + (1,))
    p = jax.nn.softmax(jnp.concatenate([s, sink], axis=-1), axis=-1)[..., :-1]
    o = jnp.einsum('bnkgqs,bnskd->bnqkgd', p.astype(v.dtype), vw)
    return o.reshape(B, T, H * dh)


def even_mixer(h, w_in, q_a_norm, w_q_b, kv_a_norm, w_kv_b, q_nope_norm, q_rope_norm,
               k_nope_norm, k_rope_norm, na_q_norm, na_k_norm, na_rpb, w_out):
    B, T, _ = h.shape
    z = h @ w_in
    o1 = Q_LORA
    o2 = o1 + KV_LORA
    o3 = o2 + MLA_ROPE
    na_w = NA_HEADS * NA_HEAD_DIM
    o4 = o3 + na_w
    o5 = o4 + na_w
    mla_out = mla_attention(z[..., :o1], z[..., o1:o2], z[..., o2:o3], q_a_norm, w_q_b,
                            kv_a_norm, w_kv_b, q_nope_norm, q_rope_norm, k_nope_norm, k_rope_norm)
    nq = rms_norm(z[..., o3:o4].reshape(B, T, NA_HEADS, NA_HEAD_DIM), na_q_norm)
    nk = rms_norm(z[..., o4:o5].reshape(B, T, NA_HEADS, NA_HEAD_DIM), na_k_norm)
    nv = z[..., o5:].reshape(B, T, NA_HEADS, NA_HEAD_DIM)
    na_out = neighbourhood_attention(nq, nk, nv, na_rpb)
    return jnp.concatenate([mla_out, na_out], axis=-1) @ w_out


def odd_mixer(h, w_in, q_norm, k_norm, sinks, w_out):
    B, T, _ = h.shape
    z = h @ w_in
    nq = SWA_HEADS * SWA_HEAD_DIM
    nkv = SWA_KV_HEADS * SWA_HEAD_DIM
    q = rms_norm(z[..., :nq].reshape(B, T, SWA_HEADS, SWA_HEAD_DIM), q_norm)
    k = rms_norm(z[..., nq:nq + nkv].reshape(B, T, SWA_KV_HEADS, SWA_HEAD_DIM), k_norm)
    v = z[..., nq + nkv:].reshape(B, T, SWA_KV_HEADS, SWA_HEAD_DIM)
    cos, sin = rope_tables(T, SWA_HEAD_DIM)
    q = apply_rope(q, cos, sin)
    k = apply_rope(k, cos, sin)
    return sliding_window_attention(q, k, v, sinks) @ w_out


def squared_relu_mlp(h, w_up, w_down):
    a = jax.nn.relu(h @ w_up)
    return (a * a) @ w_down


def per_layer_embedding(x, p_i, w_gate, w_proj, g):
    return jax.nn.sigmoid(x @ w_gate) * rms_norm(p_i @ w_proj, g)


def setup_inputs(seed: int = 0) -> dict:
    key = jax.random.key(seed)
    ks = list(jax.random.split(key, 40))

    def nrm(shape, scale):
        return jax.random.normal(ks.pop(), shape, jnp.float32) * scale

    def gain(shape):
        return 1.0 + 0.1 * jax.random.normal(ks.pop(), shape, jnp.float32)

    D = D_MODEL
    return {
        'x_prompt': nrm((BATCH, SEQ, D), 1.0),
        'x_sample': nrm((DEC_BATCH, DEC_SEQ, D), 1.0),
        'p_prompt': nrm((DEPTH, BATCH, SEQ, PLE_DIM), 1.0),
        'p_sample': nrm((DEPTH, DEC_BATCH, DEC_SEQ, PLE_DIM), 1.0),
        'attn_norm': gain((DEPTH, D)),
        'mlp_norm': gain((DEPTH, D)),
        'w_up': nrm((DEPTH, D, D_FF), D ** -0.5),
        'w_down': nrm((DEPTH, D_FF, D), D_FF ** -0.5),
        'ple_gate': nrm((DEPTH, D, D), D ** -0.5),
        'ple_proj': nrm((DEPTH, PLE_DIM, D), PLE_DIM ** -0.5),
        'ple_norm': gain((DEPTH, D)),
        'ev_w_in': nrm((N_EVEN, D, EVEN_IN), D ** -0.5),
        'mla_q_a_norm': gain((N_EVEN, Q_LORA)),
        'mla_w_q_b': nrm((N_EVEN, Q_LORA, MLA_HEADS * MLA_QK), Q_LORA ** -0.5),
        'mla_kv_a_norm': gain((N_EVEN, KV_LORA)),
        'mla_w_kv_b': nrm((N_EVEN, KV_LORA, MLA_HEADS * (MLA_NOPE + MLA_V)), KV_LORA ** -0.5),
        'mla_q_nope_norm': gain((N_EVEN, MLA_NOPE)),
        'mla_q_rope_norm': gain((N_EVEN, MLA_ROPE)),
        'mla_k_nope_norm': gain((N_EVEN, MLA_NOPE)),
        'mla_k_rope_norm': gain((N_EVEN, MLA_ROPE)),
        'na_q_norm': gain((N_EVEN, NA_HEAD_DIM)),
        'na_k_norm': gain((N_EVEN, NA_HEAD_DIM)),
        'na_rpb': nrm((N_EVEN, NA_HEADS, 2 * NA_KH_MAX - 1, 2 * NA_KW - 1), 0.5),
        'ev_w_out': nrm((N_EVEN, EVEN_MIX, D), EVEN_MIX ** -0.5),
        'od_w_in': nrm((N_ODD, D, ODD_IN), D ** -0.5),
        'swa_q_norm': gain((N_ODD, SWA_HEAD_DIM)),
        'swa_k_norm': gain((N_ODD, SWA_HEAD_DIM)),
        'swa_sinks': nrm((N_ODD, SWA_HEADS), 1.0),
        'od_w_out': nrm((N_ODD, ODD_MIX, D), ODD_MIX ** -0.5),
    }


def reference(x_prompt, x_sample, p_prompt, p_sample, attn_norm, mlp_norm, w_up, w_down,
              ple_gate, ple_proj, ple_norm, ev_w_in, mla_q_a_norm, mla_w_q_b, mla_kv_a_norm,
              mla_w_kv_b, mla_q_nope_norm, mla_q_rope_norm, mla_k_nope_norm, mla_k_rope_norm,
              na_q_norm, na_k_norm, na_rpb, ev_w_out, od_w_in, swa_q_norm, swa_k_norm,
              swa_sinks, od_w_out):
    def run(x, p):
        for i in range(DEPTH):
            j = i // 2
            h = rms_norm(x, attn_norm[i])
            if i % 2 == 0:
                mix = even_mixer(h, ev_w_in[j], mla_q_a_norm[j], mla_w_q_b[j], mla_kv_a_norm[j],
                                 mla_w_kv_b[j], mla_q_nope_norm[j], mla_q_rope_norm[j],
                                 mla_k_nope_norm[j], mla_k_rope_norm[j], na_q_norm[j],
                                 na_k_norm[j], na_rpb[j], ev_w_out[j])
            else:
                mix = odd_mixer(h, od_w_in[j], swa_q_norm[j], swa_k_norm[j], swa_sinks[j],
                                od_w_out[j])
            x = x + mix
            x = x + squared_relu_mlp(rms_norm(x, mlp_norm[i]), w_up[i], w_down[i])
            x = x + per_layer_embedding(x, p[i], ple_gate[i], ple_proj[i], ple_norm[i])
        return x

    y_prompt = run(x_prompt, p_prompt)
    y_sample = run(x_sample, p_sample)
    return (y_prompt, y_sample)
```

```python
import functools

import jax
import jax.numpy as jnp
from jax import lax
from jax.experimental import pallas as pl
from jax.experimental.pallas import tpu as pltpu

F32 = jnp.float32
BF16 = jnp.bfloat16

D_MODEL = 2048
GRID_W = 64
PLE_DIM = 256
D_FF = 4 * D_MODEL
ROPE_THETA = 10000.0
EPS = 1e-6

MLA_HEADS = 8
MLA_NOPE = 128
MLA_ROPE = 64
MLA_QK = MLA_NOPE + MLA_ROPE
MLA_V = 128
Q_LORA = 512
KV_LORA = 256

NA_HEADS = 8
NA_HEAD_DIM = 128
NA_KH = 8
NA_KW = 16

SWA_HEADS = 16
SWA_KV_HEADS = 4
SWA_HEAD_DIM = 128
SWA_WINDOW = 128

LANES = 128
MLA_QK_PAD = 2 * LANES
LAT_W = Q_LORA + KV_LORA + LANES
NEG = -1e30

NA_QROWS = 4
NA_KROWS = 12
NA_BQ = NA_QROWS * GRID_W
NA_BK = NA_KROWS * GRID_W

SWA_BQ = 256
SWA_BK = SWA_BQ + 2 * SWA_WINDOW

VMEM_LIMIT = 56 * 1024 * 1024


def _cparams(sem):
    return pltpu.CompilerParams(dimension_semantics=sem, vmem_limit_bytes=VMEM_LIMIT)


def _rms(xf, g):
    ms = jnp.mean(xf * xf, axis=-1, keepdims=True)
    return xf * lax.rsqrt(ms + EPS) * g


def _rot(y, cos, sin):
    return y * cos + pltpu.roll(y, LANES // 2, 1) * sin


def _even_lat_kernel(x_ref, g_ref, wlat_ref, qa_ref, kva_ref, krg_ref, wq_ref, qng_ref, qrg_ref,
                     wkv_ref, kng_ref, cos_ref, sin_ref, q_out, k_out, v_out):
    h = _rms(x_ref[...], g_ref[...]).astype(BF16)
    zl = jnp.dot(h, wlat_ref[...], preferred_element_type=F32)
    ql = _rms(zl[:, :Q_LORA], qa_ref[...]).astype(BF16)
    kvl = _rms(zl[:, Q_LORA:Q_LORA + KV_LORA], kva_ref[...]).astype(BF16)
    cos = cos_ref[...]
    sin = sin_ref[...]

    def rope64(x, g):
        ms = jnp.sum(x * x, axis=-1, keepdims=True) * (1.0 / MLA_ROPE)
        return _rot(x * lax.rsqrt(ms + EPS) * g, cos, sin)

    kr = rope64(zl[:, Q_LORA + KV_LORA:], krg_ref[...]).astype(BF16)
    q = jnp.dot(ql, wq_ref[...], preferred_element_type=F32)
    kv = jnp.dot(kvl, wkv_ref[...], preferred_element_type=F32)
    for hh in range(MLA_HEADS):
        o = hh * MLA_QK_PAD
        q_out[:, o:o + LANES] = _rms(q[:, o:o + LANES], qng_ref[...]).astype(BF16)
        q_out[:, o + LANES:o + 2 * LANES] = rope64(q[:, o + LANES:o + 2 * LANES], qrg_ref[...]).astype(BF16)
        k_out[:, o:o + LANES] = _rms(kv[:, o:o + LANES], kng_ref[...]).astype(BF16)
        k_out[:, o + LANES:o + 2 * LANES] = kr
        v_out[:, hh * LANES:(hh + 1) * LANES] = kv[:, o + LANES:o + 2 * LANES].astype(BF16)


def _even_lat(x, g, wlat, qa, kva, krg, wq, qng, qrg, wkv, kng, cos, sin, T, bm=512):
    N = x.shape[0]
    nt = T // bm
    row = lambda i: (i, 0)
    const = lambda i: (0, 0)
    full = lambda a: pl.BlockSpec(a.shape, const)
    return pl.pallas_call(
        _even_lat_kernel,
        grid=(N // bm,),
        in_specs=[pl.BlockSpec((bm, D_MODEL), row), full(g), full(wlat), full(qa), full(kva), full(krg),
                  full(wq), full(qng), full(qrg), full(wkv), full(kng),
                  pl.BlockSpec((bm, LANES), lambda i: (i % nt, 0)),
                  pl.BlockSpec((bm, LANES), lambda i: (i % nt, 0))],
        out_specs=[pl.BlockSpec((bm, MLA_HEADS * MLA_QK_PAD), row),
                   pl.BlockSpec((bm, MLA_HEADS * MLA_QK_PAD), row),
                   pl.BlockSpec((bm, MLA_HEADS * MLA_V), row)],
        out_shape=[jax.ShapeDtypeStruct((N, MLA_HEADS * MLA_QK_PAD), BF16),
                   jax.ShapeDtypeStruct((N, MLA_HEADS * MLA_QK_PAD), BF16),
                   jax.ShapeDtypeStruct((N, MLA_HEADS * MLA_V), BF16)],
        compiler_params=_cparams(("arbitrary",)),
        name="even_latent_proj",
    )(x, g, wlat, qa, kva, krg, wq, qng, qrg, wkv, kng, cos, sin)


def _norm_proj_kernel(modes, use_rope, *refs):
    if use_rope:
        x_ref, g_ref, w_ref, gains_ref, cos_ref, sin_ref, o_ref, h_sc = refs
    else:
        x_ref, g_ref, w_ref, gains_ref, o_ref, h_sc = refs
    j = pl.program_id(1)

    @pl.when(j == 0)
    def _():
        h_sc[...] = _rms(x_ref[...], g_ref[...]).astype(BF16)

    acc = jnp.dot(h_sc[...], w_ref[0], preferred_element_type=F32)

    def epilogue(mode_list):
        for s, mode in enumerate(mode_list):
            sl = slice(s * LANES, (s + 1) * LANES)
            a = acc[:, sl]
            if mode != "plain":
                a = _rms(a, gains_ref[0, :, sl])
                if mode == "norm_rope":
                    a = _rot(a, cos_ref[...], sin_ref[...])
            o_ref[0, :, sl] = a.astype(BF16)

    groups = {}
    for jj, ml in enumerate(modes):
        groups.setdefault(tuple(ml), []).append(jj)
    for ml, jjs in groups.items():
        cond = functools.reduce(jnp.logical_or, [j == jj for jj in jjs])
        pl.when(cond)(functools.partial(epilogue, ml))


def _norm_proj(x, g, w, gains, modes, T, cos=None, sin=None, bm=512):
    N = x.shape[0]
    nb, _, bn = w.shape
    nt = T // bm
    use_rope = cos is not None
    in_specs = [pl.BlockSpec((bm, D_MODEL), lambda i, j: (i, 0)),
                pl.BlockSpec(g.shape, lambda i, j: (0, 0)),
                pl.BlockSpec((1, D_MODEL, bn), lambda i, j: (j, 0, 0)),
                pl.BlockSpec((1, 1, bn), lambda i, j: (j, 0, 0))]
    args = [x, g, w, gains]
    if use_rope:
        in_specs += [pl.BlockSpec((bm, LANES), lambda i, j: (i % nt, 0))] * 2
        args += [cos, sin]
    return pl.pallas_call(
        functools.partial(_norm_proj_kernel, modes, use_rope),
        grid=(N // bm, nb),
        in_specs=in_specs,
        out_specs=pl.BlockSpec((1, bm, bn), lambda i, j: (j, i, 0)),
        out_shape=jax.ShapeDtypeStruct((nb, N, bn), BF16),
        scratch_shapes=[pltpu.VMEM((bm, D_MODEL), BF16)],
        compiler_params=_cparams(("arbitrary", "arbitrary")),
        name="norm_proj",
    )(*args)


def _mla_attn_kernel(q_ref, k_ref, v_ref, o_ref):
    s = lax.dot_general(q_ref[0], k_ref[0], (((1,), (1,)), ((), ())), preferred_element_type=F32)
    m = jnp.max(s, axis=-1, keepdims=True)
    p = jnp.exp(s - m)
    l = jnp.sum(p, axis=-1, keepdims=True)
    o = jnp.dot(p.astype(BF16), v_ref[0], preferred_element_type=F32)
    o_ref[0] = (o / l).astype(BF16)


def _mla_attn(q, k, v, B, T, bq=512):
    q = q.reshape(B, T, MLA_HEADS * MLA_QK_PAD)
    k = k.reshape(B, T, MLA_HEADS * MLA_QK_PAD)
    v = v.reshape(B, T, MLA_HEADS * MLA_V)
    out = pl.pallas_call(
        _mla_attn_kernel,
        grid=(B, MLA_HEADS, T // bq),
        in_specs=[pl.BlockSpec((1, bq, MLA_QK_PAD), lambda b, h, i: (b, i, h)),
                  pl.BlockSpec((1, T, MLA_QK_PAD), lambda b, h, i: (b, 0, h)),
                  pl.BlockSpec((1, T, MLA_V), lambda b, h, i: (b, 0, h))],
        out_specs=pl.BlockSpec((1, bq, MLA_V), lambda b, h, i: (b, i, h)),
        out_shape=jax.ShapeDtypeStruct((B, T, MLA_HEADS * MLA_V), BF16),
        compiler_params=_cparams(("arbitrary", "arbitrary", "arbitrary")),
        name="mla_attention",
    )(q, k, v)
    return out.reshape(B * T, MLA_HEADS * MLA_V)


def _na_attn_kernel(rows, q_ref, k_ref, v_ref, tab_ref, o_ref):
    nblk = rows // NA_QROWS

    def body(rb, carry):
        ks = jnp.clip(rb * NA_QROWS - NA_KH // 2, 0, rows - NA_KROWS)
        kind = jnp.where(rb == 0, 0, jnp.where(rb == nblk - 1, 2, 1))
        q0 = pl.multiple_of(rb * NA_BQ, NA_BQ)
        k0 = pl.multiple_of(ks * GRID_W, GRID_W)
        q = q_ref[0, pl.ds(q0, NA_BQ), :]
        k = k_ref[0, pl.ds(k0, NA_BK), :]
        v = v_ref[0, pl.ds(k0, NA_BK), :]
        s = lax.dot_general(q, k, (((1,), (1,)), ((), ())), preferred_element_type=F32)
        s = s + tab_ref[kind, 0]
        m = jnp.max(s, axis=-1, keepdims=True)
        p = jnp.exp(s - m)
        l = jnp.sum(p, axis=-1, keepdims=True)
        o = jnp.dot(p.astype(BF16), v, preferred_element_type=F32)
        o_ref[pl.ds(q0, NA_BQ), :] = (o / l).astype(BF16)
        return carry

    lax.fori_loop(0, nblk, body, 0)


def _na_attn(qkv, table, B, T):
    rows = T // GRID_W
    N = B * T
    return pl.pallas_call(
        functools.partial(_na_attn_kernel, rows),
        grid=(B, NA_HEADS),
        in_specs=[pl.BlockSpec((1, T, NA_HEAD_DIM), lambda b, h: (0, b, h)),
                  pl.BlockSpec((1, T, NA_HEAD_DIM), lambda b, h: (1, b, h)),
                  pl.BlockSpec((1, T, NA_HEAD_DIM), lambda b, h: (2, b, h)),
                  pl.BlockSpec((3, 1, NA_BQ, NA_BK), lambda b, h: (0, h, 0, 0))],
        out_specs=pl.BlockSpec((T, NA_HEAD_DIM), lambda b, h: (b, h)),
        out_shape=jax.ShapeDtypeStruct((N, NA_HEADS * NA_HEAD_DIM), BF16),
        compiler_params=_cparams(("arbitrary", "arbitrary")),
        name="na_attention",
    )(qkv, qkv, qkv, table)


def _na_bias_table(rpb):
    j = jnp.arange(NA_QROWS)[:, None, None, None]
    c = jnp.arange(GRID_W)[None, :, None, None]
    i = jnp.arange(NA_KROWS)[None, None, :, None]
    kc = jnp.arange(GRID_W)[None, None, None, :]
    c_start = jnp.clip(c - NA_KW // 2, 0, GRID_W - NA_KW)
    col_ok = (kc >= c_start) & (kc < c_start + NA_KW)
    dc = jnp.clip(kc - c + (NA_KW - 1), 0, 2 * NA_KW - 2)
    tabs = []
    for delta0 in (0, NA_KH // 2, NA_KH):
        r_rel = delta0 + j
        r_start = jnp.clip(r_rel - NA_KH // 2, 0, NA_KROWS - NA_KH)
        row_ok = (i >= r_start) & (i < r_start + NA_KH)
        dr = jnp.clip(i - r_rel + (NA_KH - 1), 0, 2 * NA_KH - 2)
        ok = jnp.broadcast_to(row_ok & col_ok, (NA_QROWS, GRID_W, NA_KROWS, GRID_W))
        drb = jnp.broadcast_to(dr, ok.shape)
        dcb = jnp.broadcast_to(dc, ok.shape)
        bias = rpb[:, drb, dcb]
        t = jnp.where(ok[None], bias, NEG)
        tabs.append(t.reshape(NA_HEADS, NA_BQ, NA_BK))
    return jnp.stack(tabs, axis=0).astype(F32)


def _swa_attn_kernel(T, sinks_ref, q_ref, k_ref, v_ref, o_ref):
    kh = pl.program_id(1)
    G = SWA_HEADS // SWA_KV_HEADS
    nblk = T // SWA_BQ
    r = lax.broadcasted_iota(jnp.int32, (SWA_BQ, SWA_BK), 0)
    c = lax.broadcasted_iota(jnp.int32, (SWA_BQ, SWA_BK), 1)
    rel = r - c

    def body(qi, carry):
        qs = qi * SWA_BQ
        ws = jnp.clip(qs - SWA_WINDOW, 0, T - SWA_BK)
        delta = qs - ws
        q0 = pl.multiple_of(qs, SWA_BQ)
        w0 = pl.multiple_of(ws, SWA_WINDOW)
        qb = q_ref[0, pl.ds(q0, SWA_BQ), :]
        q = jnp.concatenate([qb[:, g * LANES:(g + 1) * LANES] for g in range(G)], axis=0)
        k = k_ref[0, pl.ds(w0, SWA_BK), :]
        v = v_ref[0, pl.ds(w0, SWA_BK), :]
        s = lax.dot_general(q, k, (((1,), (1,)), ((), ())), preferred_element_type=F32)
        ok = jnp.abs(rel + delta) <= SWA_WINDOW
        ps, ls = [], []
        for g in range(G):
            sink = sinks_ref[kh * G + g]
            sg = jnp.where(ok, s[g * SWA_BQ:(g + 1) * SWA_BQ], NEG)
            m = jnp.maximum(jnp.max(sg, axis=-1, keepdims=True), sink)
            p = jnp.exp(sg - m)
            ls.append(jnp.sum(p, axis=-1, keepdims=True) + jnp.exp(sink - m))
            ps.append(p.astype(BF16))
        o = jnp.dot(jnp.concatenate(ps, axis=0), v, preferred_element_type=F32)
        for g in range(G):
            og = o[g * SWA_BQ:(g + 1) * SWA_BQ] / ls[g]
            o_ref[pl.ds(q0, SWA_BQ), g * LANES:(g + 1) * LANES] = og.astype(BF16)
        return carry

    lax.fori_loop(0, nblk, body, 0)


def _swa_attn(qkv, sinks, B, T):
    N = B * T
    G = SWA_HEADS // SWA_KV_HEADS
    gw = G * SWA_HEAD_DIM
    per_blk = 1024 // gw
    return pl.pallas_call(
        functools.partial(_swa_attn_kernel, T),
        grid=(B, SWA_KV_HEADS),
        in_specs=[pl.BlockSpec(memory_space=pltpu.SMEM),
                  pl.BlockSpec((1, T, gw), lambda b, h: (h // per_blk, b, h % per_blk)),
                  pl.BlockSpec((1, T, SWA_HEAD_DIM), lambda b, h: (2, b, h)),
                  pl.BlockSpec((1, T, SWA_HEAD_DIM), lambda b, h: (2, b, SWA_KV_HEADS + h))],
        out_specs=pl.BlockSpec((T, gw), lambda b, h: (b, h)),
        out_shape=jax.ShapeDtypeStruct((N, SWA_HEADS * SWA_HEAD_DIM), BF16),
        compiler_params=_cparams(("arbitrary", "arbitrary")),
        name="swa_attention",
    )(sinks, qkv, qkv, qkv)


def _outproj_kernel(n_in, x_ref, *refs):
    mix_refs = refs[:n_in]
    w_ref = refs[n_in]
    o_ref = refs[n_in + 1]
    acc = x_ref[...]
    off = 0
    for m_ref in mix_refs:
        kdim = m_ref.shape[1]
        acc = acc + jnp.dot(m_ref[...], w_ref[off:off + kdim, :], preferred_element_type=F32)
        off += kdim
    o_ref[...] = acc


def _outproj(x, mixes, w, bm=512):
    N = x.shape[0]
    row = lambda i: (i, 0)
    return pl.pallas_call(
        functools.partial(_outproj_kernel, len(mixes)),
        grid=(N // bm,),
        in_specs=[pl.BlockSpec((bm, D_MODEL), row)]
                 + [pl.BlockSpec((bm, m.shape[1]), row) for m in mixes]
                 + [pl.BlockSpec(w.shape, lambda i: (0, 0))],
        out_specs=pl.BlockSpec((bm, D_MODEL), row),
        out_shape=jax.ShapeDtypeStruct((N, D_MODEL), F32),
        compiler_params=_cparams(("arbitrary",)),
        name="out_proj",
    )(x, *mixes, w)


def _mlp_kernel(nf, x_ref, g_ref, wu_ref, wd_ref, o_ref, h_sc, acc_sc):
    f = pl.program_id(1)

    @pl.when(f == 0)
    def _():
        h_sc[...] = _rms(x_ref[...], g_ref[...]).astype(BF16)
        acc_sc[...] = jnp.zeros_like(acc_sc)

    a = jnp.dot(h_sc[...], wu_ref[...], preferred_element_type=F32)
    a = jnp.maximum(a, 0.0)
    a = (a * a).astype(BF16)
    acc_sc[...] += jnp.dot(a, wd_ref[...], preferred_element_type=F32)

    @pl.when(f == nf - 1)
    def _():
        o_ref[...] = x_ref[...] + acc_sc[...]


def _mlp(x, g, wu, wd, bm=512, bf=1024):
    N = x.shape[0]
    nf = D_FF // bf
    return pl.pallas_call(
        functools.partial(_mlp_kernel, nf),
        grid=(N // bm, nf),
        in_specs=[pl.BlockSpec((bm, D_MODEL), lambda i, f: (i, 0)),
                  pl.BlockSpec(g.shape, lambda i, f: (0, 0)),
                  pl.BlockSpec((D_MODEL, bf), lambda i, f: (0, f)),
                  pl.BlockSpec((bf, D_MODEL), lambda i, f: (f, 0))],
        out_specs=pl.BlockSpec((bm, D_MODEL), lambda i, f: (i, 0)),
        out_shape=jax.ShapeDtypeStruct((N, D_MODEL), F32),
        scratch_shapes=[pltpu.VMEM((bm, D_MODEL), BF16), pltpu.VMEM((bm, D_MODEL), F32)],
        compiler_params=_cparams(("arbitrary", "arbitrary")),
        name="sqrelu_mlp",
    )(x, g, wu, wd)


def _ple_kernel(x_ref, p_ref, wg_ref, wp_ref, g_ref, o_ref):
    x = x_ref[...]
    e = jnp.dot(p_ref[...].astype(BF16), wp_ref[...], preferred_element_type=F32)
    e = _rms(e, g_ref[...])
    z = jnp.dot(x.astype(BF16), wg_ref[...], preferred_element_type=F32)
    gate = 1.0 / (1.0 + jnp.exp(-z))
    o_ref[...] = x + gate * e


def _ple(x, p, wg, wp, g, bm=512):
    N = x.shape[0]
    row = lambda i: (i, 0)
    const = lambda i: (0, 0)
    return pl.pallas_call(
        _ple_kernel,
        grid=(N // bm,),
        in_specs=[pl.BlockSpec((bm, D_MODEL), row), pl.BlockSpec((bm, PLE_DIM), row),
                  pl.BlockSpec(wg.shape, const), pl.BlockSpec(wp.shape, const),
                  pl.BlockSpec(g.shape, const)],
        out_specs=pl.BlockSpec((bm, D_MODEL), row),
        out_shape=jax.ShapeDtypeStruct((N, D_MODEL), F32),
        compiler_params=_cparams(("arbitrary",)),
        name="per_layer_embedding",
    )(x, p, wg, wp, g)


def _pad_rope_cols(w):
    half = MLA_ROPE // 2
    z = jnp.zeros(w.shape[:-1] + (half,), w.dtype)
    return jnp.concatenate([w[..., :half], z, w[..., half:], z], axis=-1)


def _rope_tables_padded(T):
    inv_freq = 1.0 / (ROPE_THETA ** (jnp.arange(0, MLA_ROPE, 2, dtype=F32) / MLA_ROPE))
    ang = jnp.arange(T, dtype=F32)[:, None] * inv_freq[None, :]
    c, s = jnp.cos(ang), jnp.sin(ang)
    z = jnp.zeros_like(c)
    return jnp.concatenate([c, z, c, z], axis=-1), jnp.concatenate([-s, z, s, z], axis=-1)


def _rope_tables_full(T):
    inv_freq = 1.0 / (ROPE_THETA ** (jnp.arange(0, SWA_HEAD_DIM, 2, dtype=F32) / SWA_HEAD_DIM))
    ang = jnp.arange(T, dtype=F32)[:, None] * inv_freq[None, :]
    c, s = jnp.cos(ang), jnp.sin(ang)
    return jnp.concatenate([c, c], axis=-1), jnp.concatenate([-s, s], axis=-1)


def _even_params(w_in, q_a_norm, w_q_b, kv_a_norm, w_kv_b, q_nope_norm, q_rope_norm,
                 k_nope_norm, k_rope_norm, na_q_norm, na_k_norm, na_rpb, w_out):
    o1 = Q_LORA
    o2 = o1 + KV_LORA
    o3 = o2 + MLA_ROPE
    na_w = NA_HEADS * NA_HEAD_DIM
    wlat = jnp.concatenate([w_in[:, :o2], _pad_rope_cols(w_in[:, o2:o3])], axis=-1).astype(BF16)
    wq3 = w_q_b.reshape(Q_LORA, MLA_HEADS, MLA_QK)
    wq = jnp.concatenate([wq3[..., :MLA_NOPE], _pad_rope_cols(wq3[..., MLA_NOPE:])], axis=-1)
    wq = wq.reshape(Q_LORA, MLA_HEADS * MLA_QK_PAD).astype(BF16)
    mla_scale = MLA_QK ** -0.5
    na_scale = NA_HEAD_DIM ** -0.5
    wna = jnp.stack([w_in[:, o3:o3 + na_w], w_in[:, o3 + na_w:o3 + 2 * na_w],
                     w_in[:, o3 + 2 * na_w:]], axis=0).astype(BF16)
    na_gains = jnp.stack([jnp.tile(na_q_norm * na_scale, NA_HEADS), jnp.tile(na_k_norm, NA_HEADS),
                          jnp.ones((na_w,), F32)], axis=0)[:, None, :]
    return dict(
        wlat=wlat, qa=q_a_norm[None], kva=kv_a_norm[None], krg=_pad_rope_cols(k_rope_norm)[None],
        wq=wq, qng=(q_nope_norm * mla_scale)[None], qrg=(_pad_rope_cols(q_rope_norm) * mla_scale)[None],
        wkv=w_kv_b.astype(BF16), kng=k_nope_norm[None],
        wna=wna, na_gains=na_gains, table=_na_bias_table(na_rpb), w_out=w_out.astype(BF16))


def _odd_params(w_in, q_norm, k_norm, sinks, w_out):
    scale = SWA_HEAD_DIM ** -0.5
    w = jnp.stack([w_in[:, :1024], w_in[:, 1024:2048], w_in[:, 2048:]], axis=0).astype(BF16)
    gq = jnp.tile(q_norm * scale, 8)
    gkv = jnp.concatenate([jnp.tile(k_norm, SWA_KV_HEADS), jnp.ones((SWA_KV_HEADS * SWA_HEAD_DIM,), F32)])
    gains = jnp.stack([gq, gq, gkv], axis=0)[:, None, :]
    return dict(w=w, gains=gains, sinks=sinks.astype(F32), w_out=w_out.astype(BF16))


_NA_MODES = (("norm",) * 8, ("norm",) * 8, ("plain",) * 8)
_SWA_MODES = (("norm_rope",) * 8, ("norm_rope",) * 8, ("norm_rope",) * 4 + ("plain",) * 4)


def kernel(x_prompt, x_sample, p_prompt, p_sample, attn_norm, mlp_norm, w_up, w_down, ple_gate, ple_proj, ple_norm, ev_w_in, mla_q_a_norm, mla_w_q_b, mla_kv_a_norm, mla_w_kv_b, mla_q_nope_norm, mla_q_rope_norm, mla_k_nope_norm, mla_k_rope_norm, na_q_norm, na_k_norm, na_rpb, ev_w_out, od_w_in, swa_q_norm, swa_k_norm, swa_sinks, od_w_out):
    depth = attn_norm.shape[0]
    even = [_even_params(ev_w_in[j], mla_q_a_norm[j], mla_w_q_b[j], mla_kv_a_norm[j], mla_w_kv_b[j],
                         mla_q_nope_norm[j], mla_q_rope_norm[j], mla_k_nope_norm[j], mla_k_rope_norm[j],
                         na_q_norm[j], na_k_norm[j], na_rpb[j], ev_w_out[j])
            for j in range(ev_w_in.shape[0])]
    odd = [_odd_params(od_w_in[j], swa_q_norm[j], swa_k_norm[j], swa_sinks[j], od_w_out[j])
           for j in range(od_w_in.shape[0])]
    wu = w_up.astype(BF16)
    wd = w_down.astype(BF16)
    wg = ple_gate.astype(BF16)
    wp = ple_proj.astype(BF16)

    def run(x3, p4):
        B, T, _ = x3.shape
        x = x3.reshape(B * T, D_MODEL)
        cos_e, sin_e = _rope_tables_padded(T)
        cos_o, sin_o = _rope_tables_full(T)
        for i in range(depth):
            j = i // 2
            g = attn_norm[i][None]
            if i % 2 == 0:
                e = even[j]
                q, k, v = _even_lat(x, g, e["wlat"], e["qa"], e["kva"], e["krg"], e["wq"], e["qng"],
                                    e["qrg"], e["wkv"], e["kng"], cos_e, sin_e, T)
                mla_out = _mla_attn(q, k, v, B, T)
                na_qkv = _norm_proj(x, g, e["wna"], e["na_gains"], _NA_MODES, T)
                na_out = _na_attn(na_qkv, e["table"], B, T)
                x = _outproj(x, [mla_out, na_out], e["w_out"])
            else:
                o = odd[j]
                qkv = _norm_proj(x, g, o["w"], o["gains"], _SWA_MODES, T, cos_o, sin_o)
                swa_out = _swa_attn(qkv, o["sinks"], B, T)
                x = _outproj(x, [swa_out], o["w_out"])
            x = _mlp(x, mlp_norm[i][None], wu[i], wd[i])
            x = _ple(x, p4[i].reshape(B * T, PLE_DIM), wg[i], wp[i], ple_norm[i][None])
        return x.reshape(B, T, D_MODEL)

    return (run(x_prompt, p_prompt), run(x_sample, p_sample))
```

```python
import functools

import jax
import jax.numpy as jnp
import numpy as np
from jax import lax
from jax.experimental import pallas as pl
from jax.experimental.pallas import tpu as pltpu

F32 = jnp.float32
BF16 = jnp.bfloat16

D_MODEL = 2048
GRID_W = 64
PLE_DIM = 256
D_FF = 4 * D_MODEL
ROPE_THETA = 10000.0
EPS = 1e-6

MLA_HEADS = 8
MLA_NOPE = 128
MLA_ROPE = 64
MLA_QK = MLA_NOPE + MLA_ROPE
MLA_V = 128
Q_LORA = 512
KV_LORA = 256

NA_HEADS = 8
NA_HEAD_DIM = 128
NA_KH = 8
NA_KW = 16

SWA_HEADS = 16
SWA_KV_HEADS = 4
SWA_HEAD_DIM = 128
SWA_WINDOW = 128

LANES = 128
MLA_QK_PAD = 2 * LANES
LAT_W = Q_LORA + KV_LORA + LANES
NEG = -1e30
LOG2E = 1.4426950408889634

NA_QROWS = 4
NA_KROWS = 12
NA_BQ = NA_QROWS * GRID_W
NA_BK = NA_KROWS * GRID_W

SWA_BQ = 256
SWA_BK = SWA_BQ + 2 * SWA_WINDOW
ATTN_UNROLL = 4

VMEM_LIMIT = 56 * 1024 * 1024


def _cparams(sem):
    return pltpu.CompilerParams(dimension_semantics=sem, vmem_limit_bytes=VMEM_LIMIT)


def _rms(xf, g):
    ms = jnp.mean(xf * xf, axis=-1, keepdims=True)
    return xf * lax.rsqrt(ms + EPS) * g


def _rot(y, cos, sin):
    return y * cos + pltpu.roll(y, LANES // 2, 1) * sin


def _even_lat_kernel(x_ref, g_ref, wlat_ref, qa_ref, kva_ref, krg_ref, wq_ref, qng_ref, qrg_ref,
                     wkv_ref, kng_ref, cos_ref, sin_ref, q_out, k_out, v_out):
    h = _rms(x_ref[...], g_ref[...]).astype(BF16)
    zl = jnp.dot(h, wlat_ref[...], preferred_element_type=F32)
    ql = _rms(zl[:, :Q_LORA], qa_ref[...]).astype(BF16)
    kvl = _rms(zl[:, Q_LORA:Q_LORA + KV_LORA], kva_ref[...]).astype(BF16)
    cos = cos_ref[...]
    sin = sin_ref[...]

    def rope64(x, g):
        ms = jnp.sum(x * x, axis=-1, keepdims=True) * (1.0 / MLA_ROPE)
        return _rot(x * lax.rsqrt(ms + EPS) * g, cos, sin)

    kr = rope64(zl[:, Q_LORA + KV_LORA:], krg_ref[...]).astype(BF16)
    q = jnp.dot(ql, wq_ref[...], preferred_element_type=F32)
    kv = jnp.dot(kvl, wkv_ref[...], preferred_element_type=F32)
    for hh in range(MLA_HEADS):
        o = hh * MLA_QK_PAD
        q_out[:, o:o + LANES] = _rms(q[:, o:o + LANES], qng_ref[...]).astype(BF16)
        q_out[:, o + LANES:o + 2 * LANES] = rope64(q[:, o + LANES:o + 2 * LANES], qrg_ref[...]).astype(BF16)
        k_out[:, o:o + LANES] = _rms(kv[:, o:o + LANES], kng_ref[...]).astype(BF16)
        k_out[:, o + LANES:o + 2 * LANES] = kr
        v_out[:, o:o + LANES] = kv[:, o + LANES:o + 2 * LANES].astype(BF16)
        v_out[:, o + LANES:o + 2 * LANES] = jnp.ones((x_ref.shape[0], LANES), BF16)


def _even_lat(x, g, wlat, qa, kva, krg, wq, qng, qrg, wkv, kng, cos, sin, T, bm=512):
    N = x.shape[0]
    nt = T // bm
    row = lambda i: (i, 0)
    const = lambda i: (0, 0)
    full = lambda a: pl.BlockSpec(a.shape, const)
    return pl.pallas_call(
        _even_lat_kernel,
        grid=(N // bm,),
        in_specs=[pl.BlockSpec((bm, D_MODEL), row), full(g), full(wlat), full(qa), full(kva), full(krg),
                  full(wq), full(qng), full(qrg), full(wkv), full(kng),
                  pl.BlockSpec((bm, LANES), lambda i: (i % nt, 0)),
                  pl.BlockSpec((bm, LANES), lambda i: (i % nt, 0))],
        out_specs=[pl.BlockSpec((bm, MLA_HEADS * MLA_QK_PAD), row),
                   pl.BlockSpec((bm, MLA_HEADS * MLA_QK_PAD), row),
                   pl.BlockSpec((bm, MLA_HEADS * MLA_QK_PAD), row)],
        out_shape=[jax.ShapeDtypeStruct((N, MLA_HEADS * MLA_QK_PAD), BF16),
                   jax.ShapeDtypeStruct((N, MLA_HEADS * MLA_QK_PAD), BF16),
                   jax.ShapeDtypeStruct((N, MLA_HEADS * MLA_QK_PAD), BF16)],
        compiler_params=_cparams(("arbitrary",)),
        name="even_latent_proj",
    )(x, g, wlat, qa, kva, krg, wq, qng, qrg, wkv, kng, cos, sin)


def _norm_proj_kernel(modes, use_rope, *refs):
    if use_rope:
        x_ref, g_ref, w_ref, gains_ref, cos_ref, sin_ref, o_ref, h_sc = refs
    else:
        x_ref, g_ref, w_ref, gains_ref, o_ref, h_sc = refs
    j = pl.program_id(1)

    @pl.when(j == 0)
    def _():
        h_sc[...] = _rms(x_ref[...], g_ref[...]).astype(BF16)

    acc = jnp.dot(h_sc[...], w_ref[0], preferred_element_type=F32)

    def epilogue(mode_list):
        for s, mode in enumerate(mode_list):
            sl = slice(s * LANES, (s + 1) * LANES)
            a = acc[:, sl]
            if mode != "plain":
                a = _rms(a, gains_ref[0, :, sl])
                if mode == "norm_rope":
                    a = _rot(a, cos_ref[...], sin_ref[...])
            o_ref[0, :, sl] = a.astype(BF16)

    groups = {}
    for jj, ml in enumerate(modes):
        groups.setdefault(tuple(ml), []).append(jj)
    for ml, jjs in groups.items():
        cond = functools.reduce(jnp.logical_or, [j == jj for jj in jjs])
        pl.when(cond)(functools.partial(epilogue, ml))


def _norm_proj(x, g, w, gains, modes, T, cos=None, sin=None, bm=512):
    N = x.shape[0]
    nb, _, bn = w.shape
    nt = T // bm
    use_rope = cos is not None
    in_specs = [pl.BlockSpec((bm, D_MODEL), lambda i, j: (i, 0)),
                pl.BlockSpec(g.shape, lambda i, j: (0, 0)),
                pl.BlockSpec((1, D_MODEL, bn), lambda i, j: (j, 0, 0)),
                pl.BlockSpec((1, 1, bn), lambda i, j: (j, 0, 0))]
    args = [x, g, w, gains]
    if use_rope:
        in_specs += [pl.BlockSpec((bm, LANES), lambda i, j: (i % nt, 0))] * 2
        args += [cos, sin]
    return pl.pallas_call(
        functools.partial(_norm_proj_kernel, modes, use_rope),
        grid=(N // bm, nb),
        in_specs=in_specs,
        out_specs=pl.BlockSpec((1, bm, bn), lambda i, j: (j, i, 0)),
        out_shape=jax.ShapeDtypeStruct((nb, N, bn), BF16),
        scratch_shapes=[pltpu.VMEM((bm, D_MODEL), BF16)],
        compiler_params=_cparams(("arbitrary", "arbitrary")),
        name="norm_proj",
    )(*args)


def _mla_attn_kernel(nsub, q_ref, k_ref, v_ref, o_ref):
    sb = q_ref.shape[1] // nsub
    for u in range(nsub):
        rows = slice(u * sb, (u + 1) * sb)
        s = lax.dot_general(q_ref[0, rows, :], k_ref[0], (((1,), (1,)), ((), ())),
                            preferred_element_type=F32)
        m = jnp.max(s, axis=-1, keepdims=True)
        p = jnp.exp2(s - m).astype(BF16)
        o = jnp.dot(p, v_ref[0], preferred_element_type=F32)
        o_ref[0, rows, :] = (o[:, :MLA_V] / o[:, MLA_V:]).astype(BF16)


def _mla_attn(q, k, v, B, T, bq=2048, nsub=8):
    q = q.reshape(B, T, MLA_HEADS * MLA_QK_PAD)
    k = k.reshape(B, T, MLA_HEADS * MLA_QK_PAD)
    v = v.reshape(B, T, MLA_HEADS * MLA_QK_PAD)
    out = pl.pallas_call(
        functools.partial(_mla_attn_kernel, nsub),
        grid=(B, MLA_HEADS, T // bq),
        in_specs=[pl.BlockSpec((1, bq, MLA_QK_PAD), lambda b, h, i: (b, i, h)),
                  pl.BlockSpec((1, T, MLA_QK_PAD), lambda b, h, i: (b, 0, h)),
                  pl.BlockSpec((1, T, MLA_QK_PAD), lambda b, h, i: (b, 0, h))],
        out_specs=pl.BlockSpec((1, bq, MLA_V), lambda b, h, i: (b, i, h)),
        out_shape=jax.ShapeDtypeStruct((B, T, MLA_HEADS * MLA_V), BF16),
        compiler_params=_cparams(("arbitrary", "arbitrary", "arbitrary")),
        name="mla_attention",
    )(q, k, v)
    return out.reshape(B * T, MLA_HEADS * MLA_V)


def _fill_v_ones(v_ref, vext_sc):
    vext_sc[:, :LANES] = v_ref[0]
    vext_sc[:, LANES:] = jnp.ones((vext_sc.shape[0], LANES), BF16)


def _na_attn_kernel(rows, q_ref, k_ref, v_ref, tab_ref, o_ref, vext_sc):
    nblk = rows // NA_QROWS
    _fill_v_ones(v_ref, vext_sc)

    def block(rb):
        ks = jnp.clip(rb * NA_QROWS - NA_KH // 2, 0, rows - NA_KROWS)
        kind = jnp.where(rb == 0, 0, jnp.where(rb == nblk - 1, 2, 1))
        q0 = pl.multiple_of(rb * NA_BQ, NA_BQ)
        k0 = pl.multiple_of(ks * GRID_W, GRID_W)
        q = q_ref[0, pl.ds(q0, NA_BQ), :]
        k = k_ref[0, pl.ds(k0, NA_BK), :]
        s = lax.dot_general(q, k, (((1,), (1,)), ((), ())), preferred_element_type=F32)
        s = s + tab_ref[kind, 0]
        m = jnp.max(s, axis=-1, keepdims=True)
        p = jnp.exp2(s - m).astype(BF16)
        o = jnp.dot(p, vext_sc[pl.ds(k0, NA_BK), :], preferred_element_type=F32)
        o_ref[pl.ds(q0, NA_BQ), :] = (o[:, :LANES] / o[:, LANES:]).astype(BF16)

    def body(it, carry):
        for u in range(ATTN_UNROLL):
            block(ATTN_UNROLL * it + u)
        return carry

    lax.fori_loop(0, nblk // ATTN_UNROLL, body, 0)


def _na_attn(qkv, table, B, T):
    rows = T // GRID_W
    N = B * T
    return pl.pallas_call(
        functools.partial(_na_attn_kernel, rows),
        grid=(B, NA_HEADS),
        in_specs=[pl.BlockSpec((1, T, NA_HEAD_DIM), lambda b, h: (0, b, h)),
                  pl.BlockSpec((1, T, NA_HEAD_DIM), lambda b, h: (1, b, h)),
                  pl.BlockSpec((1, T, NA_HEAD_DIM), lambda b, h: (2, b, h)),
                  pl.BlockSpec((3, 1, NA_BQ, NA_BK), lambda b, h: (0, h, 0, 0))],
        out_specs=pl.BlockSpec((T, NA_HEAD_DIM), lambda b, h: (b, h)),
        out_shape=jax.ShapeDtypeStruct((N, NA_HEADS * NA_HEAD_DIM), BF16),
        scratch_shapes=[pltpu.VMEM((T, 2 * LANES), BF16)],
        compiler_params=_cparams(("arbitrary", "arbitrary")),
        name="na_attention",
    )(qkv, qkv, qkv, table)


def _na_bias_table(rpb):
    pad = GRID_W - NA_KW
    padded = jnp.pad(rpb.astype(F32), ((0, 0), (0, 0), (pad, pad)))
    toep = jnp.stack([padded[..., GRID_W - 1 - c:2 * GRID_W - 1 - c] for c in range(GRID_W)], axis=-2)
    c = np.arange(GRID_W)[:, None]
    kc = np.arange(GRID_W)[None, :]
    c_start = np.clip(c - NA_KW // 2, 0, GRID_W - NA_KW)
    col_ok = (kc >= c_start) & (kc < c_start + NA_KW)
    toep = jnp.where(col_ok, toep, NEG)
    neg_tile = jnp.full((NA_HEADS, GRID_W, GRID_W), NEG, F32)
    kinds = []
    for delta0 in (0, NA_KH // 2, NA_KH):
        qrows = []
        for j in range(NA_QROWS):
            r_rel = delta0 + j
            r_start = min(max(r_rel - NA_KH // 2, 0), NA_KROWS - NA_KH)
            tiles = [toep[:, i - r_rel + NA_KH - 1] if r_start <= i < r_start + NA_KH else neg_tile
                     for i in range(NA_KROWS)]
            qrows.append(jnp.concatenate(tiles, axis=-1))
        kinds.append(jnp.concatenate(qrows, axis=-2))
    return jnp.stack(kinds, axis=0) * LOG2E


def _swa_attn_kernel(T, sinks_ref, q_ref, k_ref, v_ref, o_ref, vext_sc):
    kh = pl.program_id(1)
    G = SWA_HEADS // SWA_KV_HEADS
    nblk = T // SWA_BQ
    r = lax.broadcasted_iota(jnp.int32, (SWA_BQ, SWA_BK), 0)
    c = lax.broadcasted_iota(jnp.int32, (SWA_BQ, SWA_BK), 1)
    rel = r - c
    _fill_v_ones(v_ref, vext_sc)

    def block(qi):
        qs = qi * SWA_BQ
        ws = jnp.clip(qs - SWA_WINDOW, 0, T - SWA_BK)
        delta = qs - ws
        q0 = pl.multiple_of(qs, SWA_BQ)
        w0 = pl.multiple_of(ws, SWA_WINDOW)
        qb = q_ref[0, pl.ds(q0, SWA_BQ), :]
        q = jnp.concatenate([qb[:, g * LANES:(g + 1) * LANES] for g in range(G)], axis=0)
        k = k_ref[0, pl.ds(w0, SWA_BK), :]
        s = lax.dot_general(q, k, (((1,), (1,)), ((), ())), preferred_element_type=F32)
        ok = jnp.abs(rel + delta) <= SWA_WINDOW
        ps, sink_terms = [], []
        for g in range(G):
            sink = sinks_ref[kh * G + g]
            sg = jnp.where(ok, s[g * SWA_BQ:(g + 1) * SWA_BQ], NEG)
            m = jnp.maximum(jnp.max(sg, axis=-1, keepdims=True), sink)
            ps.append(jnp.exp2(sg - m).astype(BF16))
            sink_terms.append(jnp.exp2(sink - m))
        o = jnp.dot(jnp.concatenate(ps, axis=0), vext_sc[pl.ds(w0, SWA_BK), :],
                    preferred_element_type=F32)
        for g in range(G):
            og = o[g * SWA_BQ:(g + 1) * SWA_BQ]
            og = og[:, :LANES] / (og[:, LANES:] + sink_terms[g])
            o_ref[pl.ds(q0, SWA_BQ), g * LANES:(g + 1) * LANES] = og.astype(BF16)

    def body(it, carry):
        for u in range(ATTN_UNROLL):
            block(ATTN_UNROLL * it + u)
        return carry

    lax.fori_loop(0, nblk // ATTN_UNROLL, body, 0)


def _swa_attn(qkv, sinks, B, T):
    N = B * T
    G = SWA_HEADS // SWA_KV_HEADS
    gw = G * SWA_HEAD_DIM
    per_blk = 1024 // gw
    return pl.pallas_call(
        functools.partial(_swa_attn_kernel, T),
        grid=(B, SWA_KV_HEADS),
        in_specs=[pl.BlockSpec(memory_space=pltpu.SMEM),
                  pl.BlockSpec((1, T, gw), lambda b, h: (h // per_blk, b, h % per_blk)),
                  pl.BlockSpec((1, T, SWA_HEAD_DIM), lambda b, h: (2, b, h)),
                  pl.BlockSpec((1, T, SWA_HEAD_DIM), lambda b, h: (2, b, SWA_KV_HEADS + h))],
        out_specs=pl.BlockSpec((T, gw), lambda b, h: (b, h)),
        out_shape=jax.ShapeDtypeStruct((N, SWA_HEADS * SWA_HEAD_DIM), BF16),
        scratch_shapes=[pltpu.VMEM((T, 2 * LANES), BF16)],
        compiler_params=_cparams(("arbitrary", "arbitrary")),
        name="swa_attention",
    )(sinks, qkv, qkv, qkv)


def _outproj_kernel(n_in, x_ref, *refs):
    mix_refs = refs[:n_in]
    w_ref = refs[n_in]
    o_ref = refs[n_in + 1]
    acc = x_ref[...]
    off = 0
    for m_ref in mix_refs:
        kdim = m_ref.shape[1]
        acc = acc + jnp.dot(m_ref[...], w_ref[off:off + kdim, :], preferred_element_type=F32)
        off += kdim
    o_ref[...] = acc


def _outproj(x, mixes, w, bm=512):
    N = x.shape[0]
    row = lambda i: (i, 0)
    return pl.pallas_call(
        functools.partial(_outproj_kernel, len(mixes)),
        grid=(N // bm,),
        in_specs=[pl.BlockSpec((bm, D_MODEL), row)]
                 + [pl.BlockSpec((bm, m.shape[1]), row) for m in mixes]
                 + [pl.BlockSpec(w.shape, lambda i: (0, 0))],
        out_specs=pl.BlockSpec((bm, D_MODEL), row),
        out_shape=jax.ShapeDtypeStruct((N, D_MODEL), F32),
        compiler_params=_cparams(("arbitrary",)),
        name="out_proj",
    )(x, *mixes, w)


def _mlp_kernel(nf, x_ref, g_ref, wu_ref, wd_ref, o_ref, h_sc, acc_sc):
    f = pl.program_id(1)

    @pl.when(f == 0)
    def _():
        h_sc[...] = _rms(x_ref[...], g_ref[...]).astype(BF16)
        acc_sc[...] = jnp.zeros_like(acc_sc)

    a = jnp.dot(h_sc[...], wu_ref[...], preferred_element_type=F32)
    a = jnp.maximum(a, 0.0)
    a = (a * a).astype(BF16)
    acc_sc[...] += jnp.dot(a, wd_ref[...], preferred_element_type=F32)

    @pl.when(f == nf - 1)
    def _():
        o_ref[...] = x_ref[...] + acc_sc[...]


def _mlp(x, g, wu, wd, layer, bm=512, bf=1024):
    N = x.shape[0]
    nf = D_FF // bf
    return pl.pallas_call(
        functools.partial(_mlp_kernel, nf),
        grid=(N // bm, nf),
        in_specs=[pl.BlockSpec((bm, D_MODEL), lambda i, f: (i, 0)),
                  pl.BlockSpec(g.shape, lambda i, f: (0, 0)),
                  pl.BlockSpec((None, D_MODEL, bf), lambda i, f: (layer, 0, f)),
                  pl.BlockSpec((None, bf, D_MODEL), lambda i, f: (layer, f, 0))],
        out_specs=pl.BlockSpec((bm, D_MODEL), lambda i, f: (i, 0)),
        out_shape=jax.ShapeDtypeStruct((N, D_MODEL), F32),
        scratch_shapes=[pltpu.VMEM((bm, D_MODEL), BF16), pltpu.VMEM((bm, D_MODEL), F32)],
        compiler_params=_cparams(("arbitrary", "arbitrary")),
        name="sqrelu_mlp",
    )(x, g, wu, wd)


def _ple_kernel(x_ref, p_ref, wg_ref, wp_ref, g_ref, o_ref):
    x = x_ref[...]
    e = jnp.dot(p_ref[...].astype(BF16), wp_ref[...], preferred_element_type=F32)
    e = _rms(e, g_ref[...])
    z = jnp.dot(x.astype(BF16), wg_ref[...], preferred_element_type=F32)
    gate = 1.0 / (1.0 + jnp.exp(-z))
    o_ref[...] = x + gate * e


def _ple(x, p, wg, wp, g, layer, bm=512):
    N = x.shape[0]
    row = lambda i: (i, 0)
    const = lambda i: (0, 0)
    lconst = lambda i: (layer, 0, 0)
    return pl.pallas_call(
        _ple_kernel,
        grid=(N // bm,),
        in_specs=[pl.BlockSpec((bm, D_MODEL), row),
                  pl.BlockSpec((None, bm, PLE_DIM), lambda i: (layer, i, 0)),
                  pl.BlockSpec((None,) + wg.shape[1:], lconst),
                  pl.BlockSpec((None,) + wp.shape[1:], lconst),
                  pl.BlockSpec(g.shape, const)],
        out_specs=pl.BlockSpec((bm, D_MODEL), row),
        out_shape=jax.ShapeDtypeStruct((N, D_MODEL), F32),
        compiler_params=_cparams(("arbitrary",)),
        name="per_layer_embedding",
    )(x, p, wg, wp, g)


def _pad_rope_cols(w):
    half = MLA_ROPE // 2
    z = jnp.zeros(w.shape[:-1] + (half,), w.dtype)
    return jnp.concatenate([w[..., :half], z, w[..., half:], z], axis=-1)


def _rope_tables_padded(T):
    inv_freq = 1.0 / (ROPE_THETA ** (jnp.arange(0, MLA_ROPE, 2, dtype=F32) / MLA_ROPE))
    ang = jnp.arange(T, dtype=F32)[:, None] * inv_freq[None, :]
    c, s = jnp.cos(ang), jnp.sin(ang)
    z = jnp.zeros_like(c)
    return jnp.concatenate([c, z, c, z], axis=-1), jnp.concatenate([-s, z, s, z], axis=-1)


def _rope_tables_full(T):
    inv_freq = 1.0 / (ROPE_THETA ** (jnp.arange(0, SWA_HEAD_DIM, 2, dtype=F32) / SWA_HEAD_DIM))
    ang = jnp.arange(T, dtype=F32)[:, None] * inv_freq[None, :]
    c, s = jnp.cos(ang), jnp.sin(ang)
    return jnp.concatenate([c, c], axis=-1), jnp.concatenate([-s, s], axis=-1)


def _even_params(w_in, q_a_norm, w_q_b, kv_a_norm, w_kv_b, q_nope_norm, q_rope_norm,
                 k_nope_norm, k_rope_norm, na_q_norm, na_k_norm, na_rpb, w_out):
    o1 = Q_LORA
    o2 = o1 + KV_LORA
    o3 = o2 + MLA_ROPE
    na_w = NA_HEADS * NA_HEAD_DIM
    wlat = jnp.concatenate([w_in[:, :o2], _pad_rope_cols(w_in[:, o2:o3])], axis=-1).astype(BF16)
    wq3 = w_q_b.reshape(Q_LORA, MLA_HEADS, MLA_QK)
    wq = jnp.concatenate([wq3[..., :MLA_NOPE], _pad_rope_cols(wq3[..., MLA_NOPE:])], axis=-1)
    wq = wq.reshape(Q_LORA, MLA_HEADS * MLA_QK_PAD).astype(BF16)
    mla_scale = (MLA_QK ** -0.5) * LOG2E
    na_scale = (NA_HEAD_DIM ** -0.5) * LOG2E
    wna = jnp.stack([w_in[:, o3:o3 + na_w], w_in[:, o3 + na_w:o3 + 2 * na_w],
                     w_in[:, o3 + 2 * na_w:]], axis=0).astype(BF16)
    na_gains = jnp.stack([jnp.tile(na_q_norm * na_scale, NA_HEADS), jnp.tile(na_k_norm, NA_HEADS),
                          jnp.ones((na_w,), F32)], axis=0)[:, None, :]
    return dict(
        wlat=wlat, qa=q_a_norm[None], kva=kv_a_norm[None], krg=_pad_rope_cols(k_rope_norm)[None],
        wq=wq, qng=(q_nope_norm * mla_scale)[None], qrg=(_pad_rope_cols(q_rope_norm) * mla_scale)[None],
        wkv=w_kv_b.astype(BF16), kng=k_nope_norm[None],
        wna=wna, na_gains=na_gains, table=_na_bias_table(na_rpb), w_out=w_out.astype(BF16))


def _odd_params(w_in, q_norm, k_norm, sinks, w_out):
    scale = (SWA_HEAD_DIM ** -0.5) * LOG2E
    w = jnp.stack([w_in[:, :1024], w_in[:, 1024:2048], w_in[:, 2048:]], axis=0).astype(BF16)
    gq = jnp.tile(q_norm * scale, 8)
    gkv = jnp.concatenate([jnp.tile(k_norm, SWA_KV_HEADS), jnp.ones((SWA_KV_HEADS * SWA_HEAD_DIM,), F32)])
    gains = jnp.stack([gq, gq, gkv], axis=0)[:, None, :]
    return dict(w=w, gains=gains, sinks=sinks.astype(F32) * LOG2E, w_out=w_out.astype(BF16))


_NA_MODES = (("norm",) * 8, ("norm",) * 8, ("plain",) * 8)
_SWA_MODES = (("norm_rope",) * 8, ("norm_rope",) * 8, ("norm_rope",) * 4 + ("plain",) * 4)


def kernel(x_prompt, x_sample, p_prompt, p_sample, attn_norm, mlp_norm, w_up, w_down, ple_gate, ple_proj, ple_norm, ev_w_in, mla_q_a_norm, mla_w_q_b, mla_kv_a_norm, mla_w_kv_b, mla_q_nope_norm, mla_q_rope_norm, mla_k_nope_norm, mla_k_rope_norm, na_q_norm, na_k_norm, na_rpb, ev_w_out, od_w_in, swa_q_norm, swa_k_norm, swa_sinks, od_w_out):
    depth = attn_norm.shape[0]
    even = [_even_params(ev_w_in[j], mla_q_a_norm[j], mla_w_q_b[j], mla_kv_a_norm[j], mla_w_kv_b[j],
                         mla_q_nope_norm[j], mla_q_rope_norm[j], mla_k_nope_norm[j], mla_k_rope_norm[j],
                         na_q_norm[j], na_k_norm[j], na_rpb[j], ev_w_out[j])
            for j in range(ev_w_in.shape[0])]
    odd = [_odd_params(od_w_in[j], swa_q_norm[j], swa_k_norm[j], swa_sinks[j], od_w_out[j])
           for j in range(od_w_in.shape[0])]
    wu = w_up.astype(BF16)
    wd = w_down.astype(BF16)
    wg = ple_gate.astype(BF16)
    wp = ple_proj.astype(BF16)

    def run(x3, p4):
        B, T, _ = x3.shape
        x = x3.reshape(B * T, D_MODEL)
        cos_e, sin_e = _rope_tables_padded(T)
        cos_o, sin_o = _rope_tables_full(T)
        for i in range(depth):
            j = i // 2
            g = attn_norm[i][None]
            if i % 2 == 0:
                e = even[j]
                q, k, v = _even_lat(x, g, e["wlat"], e["qa"], e["kva"], e["krg"], e["wq"], e["qng"],
                                    e["qrg"], e["wkv"], e["kng"], cos_e, sin_e, T)
                mla_out = _mla_attn(q, k, v, B, T)
                na_qkv = _norm_proj(x, g, e["wna"], e["na_gains"], _NA_MODES, T)
                na_out = _na_attn(na_qkv, e["table"], B, T)
                x = _outproj(x, [mla_out, na_out], e["w_out"])
            else:
                o = odd[j]
                qkv = _norm_proj(x, g, o["w"], o["gains"], _SWA_MODES, T, cos_o, sin_o)
                swa_out = _swa_attn(qkv, o["sinks"], B, T)
                x = _outproj(x, [swa_out], o["w_out"])
            x = _mlp(x, mlp_norm[i][None], wu, wd, i)
            x = _ple(x, p4.reshape(depth, B * T, PLE_DIM), wg, wp, ple_norm[i][None], i)
        return x.reshape(B, T, D_MODEL)

    return (run(x_prompt, p_prompt), run(x_sample, p_sample))
```

```python
import functools

import jax
import jax.numpy as jnp
import numpy as np
from jax import lax
from jax.experimental import pallas as pl
from jax.experimental.pallas import tpu as pltpu

F32 = jnp.float32
BF16 = jnp.bfloat16

D_MODEL = 2048
GRID_W = 64
PLE_DIM = 256
D_FF = 4 * D_MODEL
ROPE_THETA = 10000.0
EPS = 1e-6

MLA_HEADS = 8
MLA_NOPE = 128
MLA_ROPE = 64
MLA_QK = MLA_NOPE + MLA_ROPE
MLA_V = 128
Q_LORA = 512
KV_LORA = 256

NA_HEADS = 8
NA_HEAD_DIM = 128
NA_KH = 8
NA_KW = 16

SWA_HEADS = 16
SWA_KV_HEADS = 4
SWA_HEAD_DIM = 128
SWA_WINDOW = 128

LANES = 128
MLA_QK_PAD = 2 * LANES
LAT_W = Q_LORA + KV_LORA + LANES
NEG = -1e30
LOG2E = 1.4426950408889634

NA_QROWS = 4
NA_KROWS = 12
NA_BQ = NA_QROWS * GRID_W
NA_BK = NA_KROWS * GRID_W

SWA_BQ = 256
SWA_BK = SWA_BQ + 2 * SWA_WINDOW
PROJ_NSUB = 2
ATTN_UNROLL = 4

VMEM_LIMIT = 56 * 1024 * 1024


def _cparams(sem):
    return pltpu.CompilerParams(dimension_semantics=sem, vmem_limit_bytes=VMEM_LIMIT)


def _rms(xf, g):
    ms = jnp.mean(xf * xf, axis=-1, keepdims=True)
    return xf * lax.rsqrt(ms + EPS) * g


def _rot(y, cos, sin):
    return y * cos + pltpu.roll(y, LANES // 2, 1) * sin


def _even_lat_kernel(x_ref, g_ref, wlat_ref, qa_ref, kva_ref, krg_ref, wq_ref, qng_ref, qrg_ref,
                     wkv_ref, kng_ref, cos_ref, sin_ref, q_out, k_out, v_out):
    sb = x_ref.shape[0] // PROJ_NSUB
    for u in range(PROJ_NSUB):
        rows = slice(u * sb, (u + 1) * sb)
        h = _rms(x_ref[rows, :], g_ref[...]).astype(BF16)
        zl = jnp.dot(h, wlat_ref[...], preferred_element_type=F32)
        ql = _rms(zl[:, :Q_LORA], qa_ref[...]).astype(BF16)
        kvl = _rms(zl[:, Q_LORA:Q_LORA + KV_LORA], kva_ref[...]).astype(BF16)
        cos = cos_ref[rows, :]
        sin = sin_ref[rows, :]

        def rope64(x, g):
            ms = jnp.sum(x * x, axis=-1, keepdims=True) * (1.0 / MLA_ROPE)
            return _rot(x * lax.rsqrt(ms + EPS) * g, cos, sin)

        kr = rope64(zl[:, Q_LORA + KV_LORA:], krg_ref[...]).astype(BF16)
        q = jnp.dot(ql, wq_ref[...], preferred_element_type=F32)
        kv = jnp.dot(kvl, wkv_ref[...], preferred_element_type=F32)
        for hh in range(MLA_HEADS):
            o = hh * MLA_QK_PAD
            q_out[rows, o:o + LANES] = _rms(q[:, o:o + LANES], qng_ref[...]).astype(BF16)
            q_out[rows, o + LANES:o + 2 * LANES] = rope64(q[:, o + LANES:o + 2 * LANES],
                                                         qrg_ref[...]).astype(BF16)
            k_out[rows, o:o + LANES] = _rms(kv[:, o:o + LANES], kng_ref[...]).astype(BF16)
            k_out[rows, o + LANES:o + 2 * LANES] = kr
            v_out[rows, o:o + LANES] = kv[:, o + LANES:o + 2 * LANES].astype(BF16)
            v_out[rows, o + LANES:o + 2 * LANES] = jnp.ones((sb, LANES), BF16)


def _even_lat(x, g, wlat, qa, kva, krg, wq, qng, qrg, wkv, kng, cos, sin, T, bm=512):
    N = x.shape[0]
    nt = T // bm
    row = lambda i: (i, 0)
    const = lambda i: (0, 0)
    full = lambda a: pl.BlockSpec(a.shape, const)
    return pl.pallas_call(
        _even_lat_kernel,
        grid=(N // bm,),
        in_specs=[pl.BlockSpec((bm, D_MODEL), row), full(g), full(wlat), full(qa), full(kva), full(krg),
                  full(wq), full(qng), full(qrg), full(wkv), full(kng),
                  pl.BlockSpec((bm, LANES), lambda i: (i % nt, 0)),
                  pl.BlockSpec((bm, LANES), lambda i: (i % nt, 0))],
        out_specs=[pl.BlockSpec((bm, MLA_HEADS * MLA_QK_PAD), row),
                   pl.BlockSpec((bm, MLA_HEADS * MLA_QK_PAD), row),
                   pl.BlockSpec((bm, MLA_HEADS * MLA_QK_PAD), row)],
        out_shape=[jax.ShapeDtypeStruct((N, MLA_HEADS * MLA_QK_PAD), BF16),
                   jax.ShapeDtypeStruct((N, MLA_HEADS * MLA_QK_PAD), BF16),
                   jax.ShapeDtypeStruct((N, MLA_HEADS * MLA_QK_PAD), BF16)],
        compiler_params=_cparams(("arbitrary",)),
        name="even_latent_proj",
    )(x, g, wlat, qa, kva, krg, wq, qng, qrg, wkv, kng, cos, sin)


def _norm_proj_kernel(modes, use_rope, *refs):
    if use_rope:
        x_ref, g_ref, w_ref, gains_ref, cos_ref, sin_ref, o_ref = refs
    else:
        x_ref, g_ref, w_ref, gains_ref, o_ref = refs
    sb = x_ref.shape[0] // PROJ_NSUB
    for u in range(PROJ_NSUB):
        rows = slice(u * sb, (u + 1) * sb)
        h = _rms(x_ref[rows, :], g_ref[...]).astype(BF16)
        for jj, mode_list in enumerate(modes):
            acc = jnp.dot(h, w_ref[jj], preferred_element_type=F32)
            for s, mode in enumerate(mode_list):
                sl = slice(s * LANES, (s + 1) * LANES)
                a = acc[:, sl]
                if mode != "plain":
                    a = _rms(a, gains_ref[jj, :, sl])
                    if mode == "norm_rope":
                        a = _rot(a, cos_ref[rows, :], sin_ref[rows, :])
                o_ref[jj, rows, sl] = a.astype(BF16)


def _norm_proj(x, g, w, gains, modes, T, cos=None, sin=None, bm=512):
    N = x.shape[0]
    nb, _, bn = w.shape
    nt = T // bm
    use_rope = cos is not None
    in_specs = [pl.BlockSpec((bm, D_MODEL), lambda i: (i, 0)),
                pl.BlockSpec(g.shape, lambda i: (0, 0)),
                pl.BlockSpec(w.shape, lambda i: (0, 0, 0)),
                pl.BlockSpec(gains.shape, lambda i: (0, 0, 0))]
    args = [x, g, w, gains]
    if use_rope:
        in_specs += [pl.BlockSpec((bm, LANES), lambda i: (i % nt, 0))] * 2
        args += [cos, sin]
    return pl.pallas_call(
        functools.partial(_norm_proj_kernel, modes, use_rope),
        grid=(N // bm,),
        in_specs=in_specs,
        out_specs=pl.BlockSpec((nb, bm, bn), lambda i: (0, i, 0)),
        out_shape=jax.ShapeDtypeStruct((nb, N, bn), BF16),
        compiler_params=_cparams(("arbitrary",)),
        name="norm_proj",
    )(*args)


def _mla_attn_kernel(nsub, q_ref, k_ref, v_ref, o_ref):
    sb = q_ref.shape[1] // nsub
    for u in range(nsub):
        rows = slice(u * sb, (u + 1) * sb)
        s = lax.dot_general(q_ref[0, rows, :], k_ref[0], (((1,), (1,)), ((), ())),
                            preferred_element_type=F32)
        m = jnp.max(s, axis=-1, keepdims=True)
        p = jnp.exp2(s - m).astype(BF16)
        o = jnp.dot(p, v_ref[0], preferred_element_type=F32)
        o_ref[0, rows, :] = (o[:, :MLA_V] / o[:, MLA_V:]).astype(BF16)


def _mla_attn(q, k, v, B, T, bq=2048, nsub=8):
    q = q.reshape(B, T, MLA_HEADS * MLA_QK_PAD)
    k = k.reshape(B, T, MLA_HEADS * MLA_QK_PAD)
    v = v.reshape(B, T, MLA_HEADS * MLA_QK_PAD)
    out = pl.pallas_call(
        functools.partial(_mla_attn_kernel, nsub),
        grid=(B, MLA_HEADS, T // bq),
        in_specs=[pl.BlockSpec((1, bq, MLA_QK_PAD), lambda b, h, i: (b, i, h)),
                  pl.BlockSpec((1, T, MLA_QK_PAD), lambda b, h, i: (b, 0, h)),
                  pl.BlockSpec((1, T, MLA_QK_PAD), lambda b, h, i: (b, 0, h))],
        out_specs=pl.BlockSpec((1, bq, MLA_V), lambda b, h, i: (b, i, h)),
        out_shape=jax.ShapeDtypeStruct((B, T, MLA_HEADS * MLA_V), BF16),
        compiler_params=_cparams(("arbitrary", "arbitrary", "arbitrary")),
        name="mla_attention",
    )(q, k, v)
    return out.reshape(B * T, MLA_HEADS * MLA_V)


def _fill_v_ones(v_ref, vext_sc):
    vext_sc[:, :LANES] = v_ref[0]
    vext_sc[:, LANES:] = jnp.ones((vext_sc.shape[0], LANES), BF16)


def _na_attn_kernel(rows, q_ref, k_ref, v_ref, tab_ref, o_ref, vext_sc):
    nblk = rows // NA_QROWS
    _fill_v_ones(v_ref, vext_sc)

    def block(rb):
        ks = jnp.clip(rb * NA_QROWS - NA_KH // 2, 0, rows - NA_KROWS)
        kind = jnp.where(rb == 0, 0, jnp.where(rb == nblk - 1, 2, 1))
        q0 = pl.multiple_of(rb * NA_BQ, NA_BQ)
        k0 = pl.multiple_of(ks * GRID_W, GRID_W)
        q = q_ref[0, pl.ds(q0, NA_BQ), :]
        k = k_ref[0, pl.ds(k0, NA_BK), :]
        s = lax.dot_general(q, k, (((1,), (1,)), ((), ())), preferred_element_type=F32)
        s = s + tab_ref[kind, 0]
        m = jnp.max(s, axis=-1, keepdims=True)
        p = jnp.exp2(s - m).astype(BF16)
        o = jnp.dot(p, vext_sc[pl.ds(k0, NA_BK), :], preferred_element_type=F32)
        o_ref[pl.ds(q0, NA_BQ), :] = (o[:, :LANES] / o[:, LANES:]).astype(BF16)

    def body(it, carry):
        for u in range(ATTN_UNROLL):
            block(ATTN_UNROLL * it + u)
        return carry

    lax.fori_loop(0, nblk // ATTN_UNROLL, body, 0)


def _na_attn(qkv, table, B, T):
    rows = T // GRID_W
    N = B * T
    return pl.pallas_call(
        functools.partial(_na_attn_kernel, rows),
        grid=(B, NA_HEADS),
        in_specs=[pl.BlockSpec((1, T, NA_HEAD_DIM), lambda b, h: (0, b, h)),
                  pl.BlockSpec((1, T, NA_HEAD_DIM), lambda b, h: (1, b, h)),
                  pl.BlockSpec((1, T, NA_HEAD_DIM), lambda b, h: (2, b, h)),
                  pl.BlockSpec((3, 1, NA_BQ, NA_BK), lambda b, h: (0, h, 0, 0))],
        out_specs=pl.BlockSpec((T, NA_HEAD_DIM), lambda b, h: (b, h)),
        out_shape=jax.ShapeDtypeStruct((N, NA_HEADS * NA_HEAD_DIM), BF16),
        scratch_shapes=[pltpu.VMEM((T, 2 * LANES), BF16)],
        compiler_params=_cparams(("arbitrary", "arbitrary")),
        name="na_attention",
    )(qkv, qkv, qkv, table)


def _na_bias_table(rpb):
    pad = GRID_W - NA_KW
    padded = jnp.pad(rpb.astype(F32), ((0, 0), (0, 0), (pad, pad)))
    toep = jnp.stack([padded[..., GRID_W - 1 - c:2 * GRID_W - 1 - c] for c in range(GRID_W)], axis=-2)
    c = np.arange(GRID_W)[:, None]
    kc = np.arange(GRID_W)[None, :]
    c_start = np.clip(c - NA_KW // 2, 0, GRID_W - NA_KW)
    col_ok = (kc >= c_start) & (kc < c_start + NA_KW)
    toep = jnp.where(col_ok, toep, NEG)
    neg_tile = jnp.full((NA_HEADS, GRID_W, GRID_W), NEG, F32)
    kinds = []
    for delta0 in (0, NA_KH // 2, NA_KH):
        qrows = []
        for j in range(NA_QROWS):
            r_rel = delta0 + j
            r_start = min(max(r_rel - NA_KH // 2, 0), NA_KROWS - NA_KH)
            tiles = [toep[:, i - r_rel + NA_KH - 1] if r_start <= i < r_start + NA_KH else neg_tile
                     for i in range(NA_KROWS)]
            qrows.append(jnp.concatenate(tiles, axis=-1))
        kinds.append(jnp.concatenate(qrows, axis=-2))
    return jnp.stack(kinds, axis=0) * LOG2E


def _swa_attn_kernel(T, sinks_ref, q_ref, k_ref, v_ref, o_ref, vext_sc):
    kh = pl.program_id(1)
    G = SWA_HEADS // SWA_KV_HEADS
    nblk = T // SWA_BQ
    r = lax.broadcasted_iota(jnp.int32, (SWA_BQ, SWA_BK), 0)
    c = lax.broadcasted_iota(jnp.int32, (SWA_BQ, SWA_BK), 1)
    rel = r - c
    _fill_v_ones(v_ref, vext_sc)

    def block(qi):
        qs = qi * SWA_BQ
        ws = jnp.clip(qs - SWA_WINDOW, 0, T - SWA_BK)
        delta = qs - ws
        q0 = pl.multiple_of(qs, SWA_BQ)
        w0 = pl.multiple_of(ws, SWA_WINDOW)
        qb = q_ref[0, pl.ds(q0, SWA_BQ), :]
        q = jnp.concatenate([qb[:, g * LANES:(g + 1) * LANES] for g in range(G)], axis=0)
        k = k_ref[0, pl.ds(w0, SWA_BK), :]
        s = lax.dot_general(q, k, (((1,), (1,)), ((), ())), preferred_element_type=F32)
        ok = jnp.abs(rel + delta) <= SWA_WINDOW
        ps, sink_terms = [], []
        for g in range(G):
            sink = sinks_ref[kh * G + g]
            sg = jnp.where(ok, s[g * SWA_BQ:(g + 1) * SWA_BQ], NEG)
            m = jnp.maximum(jnp.max(sg, axis=-1, keepdims=True), sink)
            ps.append(jnp.exp2(sg - m).astype(BF16))
            sink_terms.append(jnp.exp2(sink - m))
        o = jnp.dot(jnp.concatenate(ps, axis=0), vext_sc[pl.ds(w0, SWA_BK), :],
                    preferred_element_type=F32)
        for g in range(G):
            og = o[g * SWA_BQ:(g + 1) * SWA_BQ]
            og = og[:, :LANES] / (og[:, LANES:] + sink_terms[g])
            o_ref[pl.ds(q0, SWA_BQ), g * LANES:(g + 1) * LANES] = og.astype(BF16)

    def body(it, carry):
        for u in range(ATTN_UNROLL):
            block(ATTN_UNROLL * it + u)
        return carry

    lax.fori_loop(0, nblk // ATTN_UNROLL, body, 0)


def _swa_attn(qkv, sinks, B, T):
    N = B * T
    G = SWA_HEADS // SWA_KV_HEADS
    gw = G * SWA_HEAD_DIM
    per_blk = 1024 // gw
    return pl.pallas_call(
        functools.partial(_swa_attn_kernel, T),
        grid=(B, SWA_KV_HEADS),
        in_specs=[pl.BlockSpec(memory_space=pltpu.SMEM),
                  pl.BlockSpec((1, T, gw), lambda b, h: (h // per_blk, b, h % per_blk)),
                  pl.BlockSpec((1, T, SWA_HEAD_DIM), lambda b, h: (2, b, h)),
                  pl.BlockSpec((1, T, SWA_HEAD_DIM), lambda b, h: (2, b, SWA_KV_HEADS + h))],
        out_specs=pl.BlockSpec((T, gw), lambda b, h: (b, h)),
        out_shape=jax.ShapeDtypeStruct((N, SWA_HEADS * SWA_HEAD_DIM), BF16),
        scratch_shapes=[pltpu.VMEM((T, 2 * LANES), BF16)],
        compiler_params=_cparams(("arbitrary", "arbitrary")),
        name="swa_attention",
    )(sinks, qkv, qkv, qkv)


def _outproj_kernel(n_in, x_ref, *refs):
    mix_refs = refs[:n_in]
    w_ref = refs[n_in]
    o_ref = refs[n_in + 1]
    acc = x_ref[...]
    off = 0
    for m_ref in mix_refs:
        kdim = m_ref.shape[1]
        acc = acc + jnp.dot(m_ref[...], w_ref[off:off + kdim, :], preferred_element_type=F32)
        off += kdim
    o_ref[...] = acc


def _outproj(x, mixes, w, bm=512):
    N = x.shape[0]
    row = lambda i: (i, 0)
    return pl.pallas_call(
        functools.partial(_outproj_kernel, len(mixes)),
        grid=(N // bm,),
        in_specs=[pl.BlockSpec((bm, D_MODEL), row)]
                 + [pl.BlockSpec((bm, m.shape[1]), row) for m in mixes]
                 + [pl.BlockSpec(w.shape, lambda i: (0, 0))],
        out_specs=pl.BlockSpec((bm, D_MODEL), row),
        out_shape=jax.ShapeDtypeStruct((N, D_MODEL), F32),
        compiler_params=_cparams(("arbitrary",)),
        name="out_proj",
    )(x, *mixes, w)


def _mlp_kernel(nf, x_ref, g_ref, wu_ref, wd_ref, o_ref, h_sc, acc_sc):
    f = pl.program_id(1)

    @pl.when(f == 0)
    def _():
        h_sc[...] = _rms(x_ref[...], g_ref[...]).astype(BF16)
        acc_sc[...] = jnp.zeros_like(acc_sc)

    a = jnp.dot(h_sc[...], wu_ref[...], preferred_element_type=F32)
    a = jnp.maximum(a, 0.0)
    a = (a * a).astype(BF16)
    acc_sc[...] += jnp.dot(a, wd_ref[...], preferred_element_type=F32)

    @pl.when(f == nf - 1)
    def _():
        o_ref[...] = x_ref[...] + acc_sc[...]


def _mlp(x, g, wu, wd, layer, bm=512, bf=1024):
    N = x.shape[0]
    nf = D_FF // bf
    return pl.pallas_call(
        functools.partial(_mlp_kernel, nf),
        grid=(N // bm, nf),
        in_specs=[pl.BlockSpec((bm, D_MODEL), lambda i, f: (i, 0)),
                  pl.BlockSpec(g.shape, lambda i, f: (0, 0)),
                  pl.BlockSpec((None, D_MODEL, bf), lambda i, f: (layer, 0, f)),
                  pl.BlockSpec((None, bf, D_MODEL), lambda i, f: (layer, f, 0))],
        out_specs=pl.BlockSpec((bm, D_MODEL), lambda i, f: (i, 0)),
        out_shape=jax.ShapeDtypeStruct((N, D_MODEL), F32),
        scratch_shapes=[pltpu.VMEM((bm, D_MODEL), BF16), pltpu.VMEM((bm, D_MODEL), F32)],
        compiler_params=_cparams(("arbitrary", "arbitrary")),
        name="sqrelu_mlp",
    )(x, g, wu, wd)


def _ple_kernel(x_ref, p_ref, wg_ref, wp_ref, g_ref, o_ref):
    x = x_ref[...]
    e = jnp.dot(p_ref[...].astype(BF16), wp_ref[...], preferred_element_type=F32)
    e = _rms(e, g_ref[...])
    z = jnp.dot(x.astype(BF16), wg_ref[...], preferred_element_type=F32)
    gate = 1.0 / (1.0 + jnp.exp(-z))
    o_ref[...] = x + gate * e


def _ple(x, p, wg, wp, g, layer, bm=512):
    N = x.shape[0]
    row = lambda i: (i, 0)
    const = lambda i: (0, 0)
    lconst = lambda i: (layer, 0, 0)
    return pl.pallas_call(
        _ple_kernel,
        grid=(N // bm,),
        in_specs=[pl.BlockSpec((bm, D_MODEL), row),
                  pl.BlockSpec((None, bm, PLE_DIM), lambda i: (layer, i, 0)),
                  pl.BlockSpec((None,) + wg.shape[1:], lconst),
                  pl.BlockSpec((None,) + wp.shape[1:], lconst),
                  pl.BlockSpec(g.shape, const)],
        out_specs=pl.BlockSpec((bm, D_MODEL), row),
        out_shape=jax.ShapeDtypeStruct((N, D_MODEL), F32),
        compiler_params=_cparams(("arbitrary",)),
        name="per_layer_embedding",
    )(x, p, wg, wp, g)


def _pad_rope_cols(w):
    half = MLA_ROPE // 2
    z = jnp.zeros(w.shape[:-1] + (half,), w.dtype)
    return jnp.concatenate([w[..., :half], z, w[..., half:], z], axis=-1)


def _rope_tables_padded(T):
    inv_freq = 1.0 / (ROPE_THETA ** (jnp.arange(0, MLA_ROPE, 2, dtype=F32) / MLA_ROPE))
    ang = jnp.arange(T, dtype=F32)[:, None] * inv_freq[None, :]
    c, s = jnp.cos(ang), jnp.sin(ang)
    z = jnp.zeros_like(c)
    return jnp.concatenate([c, z, c, z], axis=-1), jnp.concatenate([-s, z, s, z], axis=-1)


def _rope_tables_full(T):
    inv_freq = 1.0 / (ROPE_THETA ** (jnp.arange(0, SWA_HEAD_DIM, 2, dtype=F32) / SWA_HEAD_DIM))
    ang = jnp.arange(T, dtype=F32)[:, None] * inv_freq[None, :]
    c, s = jnp.cos(ang), jnp.sin(ang)
    return jnp.concatenate([c, c], axis=-1), jnp.concatenate([-s, s], axis=-1)


def _even_params(w_in, q_a_norm, w_q_b, kv_a_norm, w_kv_b, q_nope_norm, q_rope_norm,
                 k_nope_norm, k_rope_norm, na_q_norm, na_k_norm, na_rpb, w_out):
    o1 = Q_LORA
    o2 = o1 + KV_LORA
    o3 = o2 + MLA_ROPE
    na_w = NA_HEADS * NA_HEAD_DIM
    wlat = jnp.concatenate([w_in[:, :o2], _pad_rope_cols(w_in[:, o2:o3])], axis=-1).astype(BF16)
    wq3 = w_q_b.reshape(Q_LORA, MLA_HEADS, MLA_QK)
    wq = jnp.concatenate([wq3[..., :MLA_NOPE], _pad_rope_cols(wq3[..., MLA_NOPE:])], axis=-1)
    wq = wq.reshape(Q_LORA, MLA_HEADS * MLA_QK_PAD).astype(BF16)
    mla_scale = (MLA_QK ** -0.5) * LOG2E
    na_scale = (NA_HEAD_DIM ** -0.5) * LOG2E
    wna = jnp.stack([w_in[:, o3:o3 + na_w], w_in[:, o3 + na_w:o3 + 2 * na_w],
                     w_in[:, o3 + 2 * na_w:]], axis=0).astype(BF16)
    na_gains = jnp.stack([jnp.tile(na_q_norm * na_scale, NA_HEADS), jnp.tile(na_k_norm, NA_HEADS),
                          jnp.ones((na_w,), F32)], axis=0)[:, None, :]
    return dict(
        wlat=wlat, qa=q_a_norm[None], kva=kv_a_norm[None], krg=_pad_rope_cols(k_rope_norm)[None],
        wq=wq, qng=(q_nope_norm * mla_scale)[None], qrg=(_pad_rope_cols(q_rope_norm) * mla_scale)[None],
        wkv=w_kv_b.astype(BF16), kng=k_nope_norm[None],
        wna=wna, na_gains=na_gains, table=_na_bias_table(na_rpb), w_out=w_out.astype(BF16))


def _odd_params(w_in, q_norm, k_norm, sinks, w_out):
    scale = (SWA_HEAD_DIM ** -0.5) * LOG2E
    w = jnp.stack([w_in[:, :1024], w_in[:, 1024:2048], w_in[:, 2048:]], axis=0).astype(BF16)
    gq = jnp.tile(q_norm * scale, 8)
    gkv = jnp.concatenate([jnp.tile(k_norm, SWA_KV_HEADS), jnp.ones((SWA_KV_HEADS * SWA_HEAD_DIM,), F32)])
    gains = jnp.stack([gq, gq, gkv], axis=0)[:, None, :]
    return dict(w=w, gains=gains, sinks=sinks.astype(F32) * LOG2E, w_out=w_out.astype(BF16))


_NA_MODES = (("norm",) * 8, ("norm",) * 8, ("plain",) * 8)
_SWA_MODES = (("norm_rope",) * 8, ("norm_rope",) * 8, ("norm_rope",) * 4 + ("plain",) * 4)


def kernel(x_prompt, x_sample, p_prompt, p_sample, attn_norm, mlp_norm, w_up, w_down, ple_gate, ple_proj, ple_norm, ev_w_in, mla_q_a_norm, mla_w_q_b, mla_kv_a_norm, mla_w_kv_b, mla_q_nope_norm, mla_q_rope_norm, mla_k_nope_norm, mla_k_rope_norm, na_q_norm, na_k_norm, na_rpb, ev_w_out, od_w_in, swa_q_norm, swa_k_norm, swa_sinks, od_w_out):
    depth = attn_norm.shape[0]
    even = [_even_params(ev_w_in[j], mla_q_a_norm[j], mla_w_q_b[j], mla_kv_a_norm[j], mla_w_kv_b[j],
                         mla_q_nope_norm[j], mla_q_rope_norm[j], mla_k_nope_norm[j], mla_k_rope_norm[j],
                         na_q_norm[j], na_k_norm[j], na_rpb[j], ev_w_out[j])
            for j in range(ev_w_in.shape[0])]
    odd = [_odd_params(od_w_in[j], swa_q_norm[j], swa_k_norm[j], swa_sinks[j], od_w_out[j])
           for j in range(od_w_in.shape[0])]
    wu = w_up.astype(BF16)
    wd = w_down.astype(BF16)
    wg = ple_gate.astype(BF16)
    wp = ple_proj.astype(BF16)

    def run(x3, p4):
        B, T, _ = x3.shape
        x = x3.reshape(B * T, D_MODEL)
        cos_e, sin_e = _rope_tables_padded(T)
        cos_o, sin_o = _rope_tables_full(T)
        for i in range(depth):
            j = i // 2
            g = attn_norm[i][None]
            if i % 2 == 0:
                e = even[j]
                q, k, v = _even_lat(x, g, e["wlat"], e["qa"], e["kva"], e["krg"], e["wq"], e["qng"],
                                    e["qrg"], e["wkv"], e["kng"], cos_e, sin_e, T)
                mla_out = _mla_attn(q, k, v, B, T)
                na_qkv = _norm_proj(x, g, e["wna"], e["na_gains"], _NA_MODES, T)
                na_out = _na_attn(na_qkv, e["table"], B, T)
                x = _outproj(x, [mla_out, na_out], e["w_out"])
            else:
                o = odd[j]
                qkv = _norm_proj(x, g, o["w"], o["gains"], _SWA_MODES, T, cos_o, sin_o)
                swa_out = _swa_attn(qkv, o["sinks"], B, T)
                x = _outproj(x, [swa_out], o["w_out"])
            x = _mlp(x, mlp_norm[i][None], wu, wd, i)
            x = _ple(x, p4.reshape(depth, B * T, PLE_DIM), wg, wp, ple_norm[i][None], i)
        return x.reshape(B, T, D_MODEL)

    return (run(x_prompt, p_prompt), run(x_sample, p_sample))
```

```python
import functools

import jax
import jax.numpy as jnp
import numpy as np
from jax import lax
from jax.experimental import pallas as pl
from jax.experimental.pallas import tpu as pltpu

F32 = jnp.float32
BF16 = jnp.bfloat16

D_MODEL = 2048
GRID_W = 64
PLE_DIM = 256
D_FF = 4 * D_MODEL
ROPE_THETA = 10000.0
EPS = 1e-6

MLA_HEADS = 8
MLA_NOPE = 128
MLA_ROPE = 64
MLA_QK = MLA_NOPE + MLA_ROPE
MLA_V = 128
Q_LORA = 512
KV_LORA = 256

NA_HEADS = 8
NA_HEAD_DIM = 128
NA_KH = 8
NA_KW = 16

SWA_HEADS = 16
SWA_KV_HEADS = 4
SWA_HEAD_DIM = 128
SWA_WINDOW = 128

LANES = 128
MLA_QK_PAD = 2 * LANES
LAT_W = Q_LORA + KV_LORA + LANES
NEG = -1e30
LOG2E = 1.4426950408889634

NA_QROWS = 4
NA_KROWS = 12
NA_BQ = NA_QROWS * GRID_W
NA_BK = NA_KROWS * GRID_W

SWA_BQ = 256
SWA_BK = SWA_BQ + 2 * SWA_WINDOW
PROJ_NSUB = 2
ATTN_UNROLL = 8

VMEM_LIMIT = 56 * 1024 * 1024


def _cparams(sem):
    return pltpu.CompilerParams(dimension_semantics=sem, vmem_limit_bytes=VMEM_LIMIT)


def _rms(xf, g):
    ms = jnp.mean(xf * xf, axis=-1, keepdims=True)
    return xf * lax.rsqrt(ms + EPS) * g


def _rot(y, cos, sin):
    return y * cos + pltpu.roll(y, LANES // 2, 1) * sin


def _even_lat_kernel(x_ref, g_ref, wlat_ref, qa_ref, kva_ref, krg_ref, wq_ref, qng_ref, qrg_ref,
                     wkv_ref, kng_ref, cos_ref, sin_ref, q_out, k_out, v_out):
    sb = x_ref.shape[0] // PROJ_NSUB
    for u in range(PROJ_NSUB):
        rows = slice(u * sb, (u + 1) * sb)
        h = _rms(x_ref[rows, :], g_ref[...]).astype(BF16)
        zl = jnp.dot(h, wlat_ref[...], preferred_element_type=F32)
        ql = _rms(zl[:, :Q_LORA], qa_ref[...]).astype(BF16)
        kvl = _rms(zl[:, Q_LORA:Q_LORA + KV_LORA], kva_ref[...]).astype(BF16)
        cos = cos_ref[rows, :]
        sin = sin_ref[rows, :]

        def rope64(x, g):
            ms = jnp.sum(x * x, axis=-1, keepdims=True) * (1.0 / MLA_ROPE)
            return _rot(x * lax.rsqrt(ms + EPS) * g, cos, sin)

        kr = rope64(zl[:, Q_LORA + KV_LORA:], krg_ref[...]).astype(BF16)
        q = jnp.dot(ql, wq_ref[...], preferred_element_type=F32)
        kv = jnp.dot(kvl, wkv_ref[...], preferred_element_type=F32)
        for hh in range(MLA_HEADS):
            o = hh * MLA_QK_PAD
            q_out[rows, o:o + LANES] = _rms(q[:, o:o + LANES], qng_ref[...]).astype(BF16)
            q_out[rows, o + LANES:o + 2 * LANES] = rope64(q[:, o + LANES:o + 2 * LANES],
                                                         qrg_ref[...]).astype(BF16)
            k_out[rows, o:o + LANES] = _rms(kv[:, o:o + LANES], kng_ref[...]).astype(BF16)
            k_out[rows, o + LANES:o + 2 * LANES] = kr
            v_out[rows, o:o + LANES] = kv[:, o + LANES:o + 2 * LANES].astype(BF16)
            v_out[rows, o + LANES:o + 2 * LANES] = jnp.ones((sb, LANES), BF16)


def _even_lat(x, g, wlat, qa, kva, krg, wq, qng, qrg, wkv, kng, cos, sin, T, bm=512):
    N = x.shape[0]
    nt = T // bm
    row = lambda i: (i, 0)
    const = lambda i: (0, 0)
    full = lambda a: pl.BlockSpec(a.shape, const)
    return pl.pallas_call(
        _even_lat_kernel,
        grid=(N // bm,),
        in_specs=[pl.BlockSpec((bm, D_MODEL), row), full(g), full(wlat), full(qa), full(kva), full(krg),
                  full(wq), full(qng), full(qrg), full(wkv), full(kng),
                  pl.BlockSpec((bm, LANES), lambda i: (i % nt, 0)),
                  pl.BlockSpec((bm, LANES), lambda i: (i % nt, 0))],
        out_specs=[pl.BlockSpec((bm, MLA_HEADS * MLA_QK_PAD), row),
                   pl.BlockSpec((bm, MLA_HEADS * MLA_QK_PAD), row),
                   pl.BlockSpec((bm, MLA_HEADS * MLA_QK_PAD), row)],
        out_shape=[jax.ShapeDtypeStruct((N, MLA_HEADS * MLA_QK_PAD), BF16),
                   jax.ShapeDtypeStruct((N, MLA_HEADS * MLA_QK_PAD), BF16),
                   jax.ShapeDtypeStruct((N, MLA_HEADS * MLA_QK_PAD), BF16)],
        compiler_params=_cparams(("arbitrary",)),
        name="even_latent_proj",
    )(x, g, wlat, qa, kva, krg, wq, qng, qrg, wkv, kng, cos, sin)


def _norm_proj_kernel(modes, use_rope, *refs):
    if use_rope:
        x_ref, g_ref, w_ref, gains_ref, cos_ref, sin_ref, o_ref = refs
    else:
        x_ref, g_ref, w_ref, gains_ref, o_ref = refs
    sb = x_ref.shape[0] // PROJ_NSUB
    for u in range(PROJ_NSUB):
        rows = slice(u * sb, (u + 1) * sb)
        h = _rms(x_ref[rows, :], g_ref[...]).astype(BF16)
        for jj, mode_list in enumerate(modes):
            acc = jnp.dot(h, w_ref[jj], preferred_element_type=F32)
            for s, mode in enumerate(mode_list):
                sl = slice(s * LANES, (s + 1) * LANES)
                a = acc[:, sl]
                if mode != "plain":
                    a = _rms(a, gains_ref[jj, :, sl])
                    if mode == "norm_rope":
                        a = _rot(a, cos_ref[rows, :], sin_ref[rows, :])
                o_ref[jj, rows, sl] = a.astype(BF16)


def _norm_proj(x, g, w, gains, modes, T, cos=None, sin=None, bm=512):
    N = x.shape[0]
    nb, _, bn = w.shape
    nt = T // bm
    use_rope = cos is not None
    in_specs = [pl.BlockSpec((bm, D_MODEL), lambda i: (i, 0)),
                pl.BlockSpec(g.shape, lambda i: (0, 0)),
                pl.BlockSpec(w.shape, lambda i: (0, 0, 0)),
                pl.BlockSpec(gains.shape, lambda i: (0, 0, 0))]
    args = [x, g, w, gains]
    if use_rope:
        in_specs += [pl.BlockSpec((bm, LANES), lambda i: (i % nt, 0))] * 2
        args += [cos, sin]
    return pl.pallas_call(
        functools.partial(_norm_proj_kernel, modes, use_rope),
        grid=(N // bm,),
        in_specs=in_specs,
        out_specs=pl.BlockSpec((nb, bm, bn), lambda i: (0, i, 0)),
        out_shape=jax.ShapeDtypeStruct((nb, N, bn), BF16),
        compiler_params=_cparams(("arbitrary",)),
        name="norm_proj",
    )(*args)


def _mla_attn_kernel(nsub, q_ref, k_ref, v_ref, o_ref):
    sb = q_ref.shape[1] // nsub
    for u in range(nsub):
        rows = slice(u * sb, (u + 1) * sb)
        s = lax.dot_general(q_ref[0, rows, :], k_ref[0], (((1,), (1,)), ((), ())),
                            preferred_element_type=F32)
        m = jnp.max(s, axis=-1, keepdims=True)
        p = jnp.exp2(s - m).astype(BF16)
        o = jnp.dot(p, v_ref[0], preferred_element_type=F32)
        o_ref[0, rows, :] = (o[:, :MLA_V] / o[:, MLA_V:]).astype(BF16)


def _mla_attn(q, k, v, B, T, bq=2048, nsub=8):
    q = q.reshape(B, T, MLA_HEADS * MLA_QK_PAD)
    k = k.reshape(B, T, MLA_HEADS * MLA_QK_PAD)
    v = v.reshape(B, T, MLA_HEADS * MLA_QK_PAD)
    out = pl.pallas_call(
        functools.partial(_mla_attn_kernel, nsub),
        grid=(B, MLA_HEADS, T // bq),
        in_specs=[pl.BlockSpec((1, bq, MLA_QK_PAD), lambda b, h, i: (b, i, h)),
                  pl.BlockSpec((1, T, MLA_QK_PAD), lambda b, h, i: (b, 0, h)),
                  pl.BlockSpec((1, T, MLA_QK_PAD), lambda b, h, i: (b, 0, h))],
        out_specs=pl.BlockSpec((1, bq, MLA_V), lambda b, h, i: (b, i, h)),
        out_shape=jax.ShapeDtypeStruct((B, T, MLA_HEADS * MLA_V), BF16),
        compiler_params=_cparams(("arbitrary", "arbitrary", "arbitrary")),
        name="mla_attention",
    )(q, k, v)
    return out.reshape(B * T, MLA_HEADS * MLA_V)


def _fill_v_ones(v_ref, vext_sc):
    vext_sc[:, :LANES] = v_ref[0]
    vext_sc[:, LANES:] = jnp.ones((vext_sc.shape[0], LANES), BF16)


def _na_attn_kernel(rows, q_ref, k_ref, v_ref, tab_ref, o_ref, vext_sc):
    nblk = rows // NA_QROWS
    _fill_v_ones(v_ref, vext_sc)

    def block(rb):
        ks = jnp.clip(rb * NA_QROWS - NA_KH // 2, 0, rows - NA_KROWS)
        kind = jnp.where(rb == 0, 0, jnp.where(rb == nblk - 1, 2, 1))
        q0 = pl.multiple_of(rb * NA_BQ, NA_BQ)
        k0 = pl.multiple_of(ks * GRID_W, GRID_W)
        q = q_ref[0, pl.ds(q0, NA_BQ), :]
        k = k_ref[0, pl.ds(k0, NA_BK), :]
        s = lax.dot_general(q, k, (((1,), (1,)), ((), ())), preferred_element_type=F32)
        s = s + tab_ref[kind, 0]
        m = jnp.max(s, axis=-1, keepdims=True)
        p = jnp.exp2(s - m).astype(BF16)
        o = jnp.dot(p, vext_sc[pl.ds(k0, NA_BK), :], preferred_element_type=F32)
        o_ref[pl.ds(q0, NA_BQ), :] = (o[:, :LANES] / o[:, LANES:]).astype(BF16)

    def body(it, carry):
        for u in range(ATTN_UNROLL):
            block(ATTN_UNROLL * it + u)
        return carry

    lax.fori_loop(0, nblk // ATTN_UNROLL, body, 0)


def _na_attn(qkv, table, B, T):
    rows = T // GRID_W
    N = B * T
    return pl.pallas_call(
        functools.partial(_na_attn_kernel, rows),
        grid=(B, NA_HEADS),
        in_specs=[pl.BlockSpec((1, T, NA_HEAD_DIM), lambda b, h: (0, b, h)),
                  pl.BlockSpec((1, T, NA_HEAD_DIM), lambda b, h: (1, b, h)),
                  pl.BlockSpec((1, T, NA_HEAD_DIM), lambda b, h: (2, b, h)),
                  pl.BlockSpec((3, 1, NA_BQ, NA_BK), lambda b, h: (0, h, 0, 0))],
        out_specs=pl.BlockSpec((T, NA_HEAD_DIM), lambda b, h: (b, h)),
        out_shape=jax.ShapeDtypeStruct((N, NA_HEADS * NA_HEAD_DIM), BF16),
        scratch_shapes=[pltpu.VMEM((T, 2 * LANES), BF16)],
        compiler_params=_cparams(("arbitrary", "arbitrary")),
        name="na_attention",
    )(qkv, qkv, qkv, table)


def _na_bias_table(rpb):
    pad = GRID_W - NA_KW
    padded = jnp.pad(rpb.astype(F32), ((0, 0), (0, 0), (pad, pad)))
    toep = jnp.stack([padded[..., GRID_W - 1 - c:2 * GRID_W - 1 - c] for c in range(GRID_W)], axis=-2)
    c = np.arange(GRID_W)[:, None]
    kc = np.arange(GRID_W)[None, :]
    c_start = np.clip(c - NA_KW // 2, 0, GRID_W - NA_KW)
    col_ok = (kc >= c_start) & (kc < c_start + NA_KW)
    toep = jnp.where(col_ok, toep, NEG)
    neg_tile = jnp.full((NA_HEADS, GRID_W, GRID_W), NEG, F32)
    kinds = []
    for delta0 in (0, NA_KH // 2, NA_KH):
        qrows = []
        for j in range(NA_QROWS):
            r_rel = delta0 + j
            r_start = min(max(r_rel - NA_KH // 2, 0), NA_KROWS - NA_KH)
            tiles = [toep[:, i - r_rel + NA_KH - 1] if r_start <= i < r_start + NA_KH else neg_tile
                     for i in range(NA_KROWS)]
            qrows.append(jnp.concatenate(tiles, axis=-1))
        kinds.append(jnp.concatenate(qrows, axis=-2))
    return jnp.stack(kinds, axis=0) * LOG2E


def _swa_attn_kernel(T, sinks_ref, q_ref, k_ref, v_ref, o_ref, vext_sc):
    kh = pl.program_id(1)
    G = SWA_HEADS // SWA_KV_HEADS
    nblk = T // SWA_BQ
    r = lax.broadcasted_iota(jnp.int32, (SWA_BQ, SWA_BK), 0)
    c = lax.broadcasted_iota(jnp.int32, (SWA_BQ, SWA_BK), 1)
    rel = r - c
    _fill_v_ones(v_ref, vext_sc)

    def block(qi):
        qs = qi * SWA_BQ
        ws = jnp.clip(qs - SWA_WINDOW, 0, T - SWA_BK)
        delta = qs - ws
        q0 = pl.multiple_of(qs, SWA_BQ)
        w0 = pl.multiple_of(ws, SWA_WINDOW)
        qb = q_ref[0, pl.ds(q0, SWA_BQ), :]
        q = jnp.concatenate([qb[:, g * LANES:(g + 1) * LANES] for g in range(G)], axis=0)
        k = k_ref[0, pl.ds(w0, SWA_BK), :]
        s = lax.dot_general(q, k, (((1,), (1,)), ((), ())), preferred_element_type=F32)
        ok = jnp.abs(rel + delta) <= SWA_WINDOW
        ps, sink_terms = [], []
        for g in range(G):
            sink = sinks_ref[kh * G + g]
            sg = jnp.where(ok, s[g * SWA_BQ:(g + 1) * SWA_BQ], NEG)
            m = jnp.maximum(jnp.max(sg, axis=-1, keepdims=True), sink)
            ps.append(jnp.exp2(sg - m).astype(BF16))
            sink_terms.append(jnp.exp2(sink - m))
        o = jnp.dot(jnp.concatenate(ps, axis=0), vext_sc[pl.ds(w0, SWA_BK), :],
                    preferred_element_type=F32)
        for g in range(G):
            og = o[g * SWA_BQ:(g + 1) * SWA_BQ]
            og = og[:, :LANES] / (og[:, LANES:] + sink_terms[g])
            o_ref[pl.ds(q0, SWA_BQ), g * LANES:(g + 1) * LANES] = og.astype(BF16)

    def body(it, carry):
        for u in range(ATTN_UNROLL):
            block(ATTN_UNROLL * it + u)
        return carry

    lax.fori_loop(0, nblk // ATTN_UNROLL, body, 0)


def _swa_attn(qkv, sinks, B, T):
    N = B * T
    G = SWA_HEADS // SWA_KV_HEADS
    gw = G * SWA_HEAD_DIM
    per_blk = 1024 // gw
    return pl.pallas_call(
        functools.partial(_swa_attn_kernel, T),
        grid=(B, SWA_KV_HEADS),
        in_specs=[pl.BlockSpec(memory_space=pltpu.SMEM),
                  pl.BlockSpec((1, T, gw), lambda b, h: (h // per_blk, b, h % per_blk)),
                  pl.BlockSpec((1, T, SWA_HEAD_DIM), lambda b, h: (2, b, h)),
                  pl.BlockSpec((1, T, SWA_HEAD_DIM), lambda b, h: (2, b, SWA_KV_HEADS + h))],
        out_specs=pl.BlockSpec((T, gw), lambda b, h: (b, h)),
        out_shape=jax.ShapeDtypeStruct((N, SWA_HEADS * SWA_HEAD_DIM), BF16),
        scratch_shapes=[pltpu.VMEM((T, 2 * LANES), BF16)],
        compiler_params=_cparams(("arbitrary", "arbitrary")),
        name="swa_attention",
    )(sinks, qkv, qkv, qkv)


def _outproj_kernel(n_in, x_ref, *refs):
    mix_refs = refs[:n_in]
    w_ref = refs[n_in]
    o_ref = refs[n_in + 1]
    acc = x_ref[...]
    off = 0
    for m_ref in mix_refs:
        kdim = m_ref.shape[1]
        acc = acc + jnp.dot(m_ref[...], w_ref[off:off + kdim, :], preferred_element_type=F32)
        off += kdim
    o_ref[...] = acc


def _outproj(x, mixes, w, bm=512):
    N = x.shape[0]
    row = lambda i: (i, 0)
    return pl.pallas_call(
        functools.partial(_outproj_kernel, len(mixes)),
        grid=(N // bm,),
        in_specs=[pl.BlockSpec((bm, D_MODEL), row)]
                 + [pl.BlockSpec((bm, m.shape[1]), row) for m in mixes]
                 + [pl.BlockSpec(w.shape, lambda i: (0, 0))],
        out_specs=pl.BlockSpec((bm, D_MODEL), row),
        out_shape=jax.ShapeDtypeStruct((N, D_MODEL), F32),
        compiler_params=_cparams(("arbitrary",)),
        name="out_proj",
    )(x, *mixes, w)


def _mlp_kernel(nf, x_ref, g_ref, wu_ref, wd_ref, o_ref, h_sc, acc_sc):
    f = pl.program_id(1)

    @pl.when(f == 0)
    def _():
        h_sc[...] = _rms(x_ref[...], g_ref[...]).astype(BF16)
        acc_sc[...] = jnp.zeros_like(acc_sc)

    a = jnp.dot(h_sc[...], wu_ref[...], preferred_element_type=F32)
    a = jnp.maximum(a, 0.0)
    a = (a * a).astype(BF16)
    acc_sc[...] += jnp.dot(a, wd_ref[...], preferred_element_type=F32)

    @pl.when(f == nf - 1)
    def _():
        o_ref[...] = x_ref[...] + acc_sc[...]


def _mlp(x, g, wu, wd, layer, bm=512, bf=1024):
    N = x.shape[0]
    nf = D_FF // bf
    return pl.pallas_call(
        functools.partial(_mlp_kernel, nf),
        grid=(N // bm, nf),
        in_specs=[pl.BlockSpec((bm, D_MODEL), lambda i, f: (i, 0)),
                  pl.BlockSpec(g.shape, lambda i, f: (0, 0)),
                  pl.BlockSpec((None, D_MODEL, bf), lambda i, f: (layer, 0, f)),
                  pl.BlockSpec((None, bf, D_MODEL), lambda i, f: (layer, f, 0))],
        out_specs=pl.BlockSpec((bm, D_MODEL), lambda i, f: (i, 0)),
        out_shape=jax.ShapeDtypeStruct((N, D_MODEL), F32),
        scratch_shapes=[pltpu.VMEM((bm, D_MODEL), BF16), pltpu.VMEM((bm, D_MODEL), F32)],
        compiler_params=_cparams(("arbitrary", "arbitrary")),
        name="sqrelu_mlp",
    )(x, g, wu, wd)


def _ple_kernel(x_ref, p_ref, wg_ref, wp_ref, g_ref, o_ref):
    sb = x_ref.shape[0] // PROJ_NSUB
    for u in range(PROJ_NSUB):
        rows = slice(u * sb, (u + 1) * sb)
        x = x_ref[rows, :]
        e = jnp.dot(p_ref[rows, :].astype(BF16), wp_ref[...], preferred_element_type=F32)
        e = _rms(e, g_ref[...])
        z = jnp.dot(x.astype(BF16), wg_ref[...], preferred_element_type=F32)
        gate = 1.0 / (1.0 + jnp.exp(-z))
        o_ref[rows, :] = x + gate * e


def _ple(x, p, wg, wp, g, layer, bm=512):
    N = x.shape[0]
    row = lambda i: (i, 0)
    const = lambda i: (0, 0)
    lconst = lambda i: (layer, 0, 0)
    return pl.pallas_call(
        _ple_kernel,
        grid=(N // bm,),
        in_specs=[pl.BlockSpec((bm, D_MODEL), row),
                  pl.BlockSpec((None, bm, PLE_DIM), lambda i: (layer, i, 0)),
                  pl.BlockSpec((None,) + wg.shape[1:], lconst),
                  pl.BlockSpec((None,) + wp.shape[1:], lconst),
                  pl.BlockSpec(g.shape, const)],
        out_specs=pl.BlockSpec((bm, D_MODEL), row),
        out_shape=jax.ShapeDtypeStruct((N, D_MODEL), F32),
        compiler_params=_cparams(("arbitrary",)),
        name="per_layer_embedding",
    )(x, p, wg, wp, g)


def _pad_rope_cols(w):
    half = MLA_ROPE // 2
    z = jnp.zeros(w.shape[:-1] + (half,), w.dtype)
    return jnp.concatenate([w[..., :half], z, w[..., half:], z], axis=-1)


def _rope_tables_padded(T):
    inv_freq = 1.0 / (ROPE_THETA ** (jnp.arange(0, MLA_ROPE, 2, dtype=F32) / MLA_ROPE))
    ang = jnp.arange(T, dtype=F32)[:, None] * inv_freq[None, :]
    c, s = jnp.cos(ang), jnp.sin(ang)
    z = jnp.zeros_like(c)
    return jnp.concatenate([c, z, c, z], axis=-1), jnp.concatenate([-s, z, s, z], axis=-1)


def _rope_tables_full(T):
    inv_freq = 1.0 / (ROPE_THETA ** (jnp.arange(0, SWA_HEAD_DIM, 2, dtype=F32) / SWA_HEAD_DIM))
    ang = jnp.arange(T, dtype=F32)[:, None] * inv_freq[None, :]
    c, s = jnp.cos(ang), jnp.sin(ang)
    return jnp.concatenate([c, c], axis=-1), jnp.concatenate([-s, s], axis=-1)


def _even_params(w_in, q_a_norm, w_q_b, kv_a_norm, w_kv_b, q_nope_norm, q_rope_norm,
                 k_nope_norm, k_rope_norm, na_q_norm, na_k_norm, na_rpb, w_out):
    o1 = Q_LORA
    o2 = o1 + KV_LORA
    o3 = o2 + MLA_ROPE
    na_w = NA_HEADS * NA_HEAD_DIM
    wlat = jnp.concatenate([w_in[:, :o2], _pad_rope_cols(w_in[:, o2:o3])], axis=-1).astype(BF16)
    wq3 = w_q_b.reshape(Q_LORA, MLA_HEADS, MLA_QK)
    wq = jnp.concatenate([wq3[..., :MLA_NOPE], _pad_rope_cols(wq3[..., MLA_NOPE:])], axis=-1)
    wq = wq.reshape(Q_LORA, MLA_HEADS * MLA_QK_PAD).astype(BF16)
    mla_scale = (MLA_QK ** -0.5) * LOG2E
    na_scale = (NA_HEAD_DIM ** -0.5) * LOG2E
    wna = jnp.stack([w_in[:, o3:o3 + na_w], w_in[:, o3 + na_w:o3 + 2 * na_w],
                     w_in[:, o3 + 2 * na_w:]], axis=0).astype(BF16)
    na_gains = jnp.stack([jnp.tile(na_q_norm * na_scale, NA_HEADS), jnp.tile(na_k_norm, NA_HEADS),
                          jnp.ones((na_w,), F32)], axis=0)[:, None, :]
    return dict(
        wlat=wlat, qa=q_a_norm[None], kva=kv_a_norm[None], krg=_pad_rope_cols(k_rope_norm)[None],
        wq=wq, qng=(q_nope_norm * mla_scale)[None], qrg=(_pad_rope_cols(q_rope_norm) * mla_scale)[None],
        wkv=w_kv_b.astype(BF16), kng=k_nope_norm[None],
        wna=wna, na_gains=na_gains, table=_na_bias_table(na_rpb), w_out=w_out.astype(BF16))


def _odd_params(w_in, q_norm, k_norm, sinks, w_out):
    scale = (SWA_HEAD_DIM ** -0.5) * LOG2E
    w = jnp.stack([w_in[:, :1024], w_in[:, 1024:2048], w_in[:, 2048:]], axis=0).astype(BF16)
    gq = jnp.tile(q_norm * scale, 8)
    gkv = jnp.concatenate([jnp.tile(k_norm, SWA_KV_HEADS), jnp.ones((SWA_KV_HEADS * SWA_HEAD_DIM,), F32)])
    gains = jnp.stack([gq, gq, gkv], axis=0)[:, None, :]
    return dict(w=w, gains=gains, sinks=sinks.astype(F32) * LOG2E, w_out=w_out.astype(BF16))


_NA_MODES = (("norm",) * 8, ("norm",) * 8, ("plain",) * 8)
_SWA_MODES = (("norm_rope",) * 8, ("norm_rope",) * 8, ("norm_rope",) * 4 + ("plain",) * 4)


def kernel(x_prompt, x_sample, p_prompt, p_sample, attn_norm, mlp_norm, w_up, w_down, ple_gate, ple_proj, ple_norm, ev_w_in, mla_q_a_norm, mla_w_q_b, mla_kv_a_norm, mla_w_kv_b, mla_q_nope_norm, mla_q_rope_norm, mla_k_nope_norm, mla_k_rope_norm, na_q_norm, na_k_norm, na_rpb, ev_w_out, od_w_in, swa_q_norm, swa_k_norm, swa_sinks, od_w_out):
    depth = attn_norm.shape[0]
    even = [_even_params(ev_w_in[j], mla_q_a_norm[j], mla_w_q_b[j], mla_kv_a_norm[j], mla_w_kv_b[j],
                         mla_q_nope_norm[j], mla_q_rope_norm[j], mla_k_nope_norm[j], mla_k_rope_norm[j],
                         na_q_norm[j], na_k_norm[j], na_rpb[j], ev_w_out[j])
            for j in range(ev_w_in.shape[0])]
    odd = [_odd_params(od_w_in[j], swa_q_norm[j], swa_k_norm[j], swa_sinks[j], od_w_out[j])
           for j in range(od_w_in.shape[0])]
    wu = w_up.astype(BF16)
    wd = w_down.astype(BF16)
    wg = ple_gate.astype(BF16)
    wp = ple_proj.astype(BF16)

    def run(x3, p4):
        B, T, _ = x3.shape
        x = x3.reshape(B * T, D_MODEL)
        cos_e, sin_e = _rope_tables_padded(T)
        cos_o, sin_o = _rope_tables_full(T)
        for i in range(depth):
            j = i // 2
            g = attn_norm[i][None]
            if i % 2 == 0:
                e = even[j]
                q, k, v = _even_lat(x, g, e["wlat"], e["qa"], e["kva"], e["krg"], e["wq"], e["qng"],
                                    e["qrg"], e["wkv"], e["kng"], cos_e, sin_e, T)
                mla_out = _mla_attn(q, k, v, B, T)
                na_qkv = _norm_proj(x, g, e["wna"], e["na_gains"], _NA_MODES, T)
                na_out = _na_attn(na_qkv, e["table"], B, T)
                x = _outproj(x, [mla_out, na_out], e["w_out"])
            else:
                o = odd[j]
                qkv = _norm_proj(x, g, o["w"], o["gains"], _SWA_MODES, T, cos_o, sin_o)
                swa_out = _swa_attn(qkv, o["sinks"], B, T)
                x = _outproj(x, [swa_out], o["w_out"])
            x = _mlp(x, mlp_norm[i][None], wu, wd, i)
            x = _ple(x, p4.reshape(depth, B * T, PLE_DIM), wg, wp, ple_norm[i][None], i)
        return x.reshape(B, T, D_MODEL)

    return (run(x_prompt, p_prompt), run(x_sample, p_sample))
```
